```python
import math
import jax, jax.numpy as jnp
from jax import lax
import numpy as np

D_MODEL = 1024
BATCH = 4
SEQ = 8192
DEPTH = 2

N_META = 16
ROPE_THETA = 500000.0
BLOCK_Q = 128
TOPK_MAX = 256
NEG = -1e30
A_HEADS = 8
A_KV_HEADS = 2
A_HEAD_DIM = 64
A_ROT = A_HEAD_DIM // 4
IDX_HEADS = 8
IDX_DIM = 64
IDX_ROT = IDX_DIM // 4
B_HEADS = 8
B_NOPE = 64
B_ROPE = 32
B_V = 64
B_Q_LORA = 256
B_KV_LORA = 128
D_FF = 2816
DN_ALPHA = (2 * DEPTH) ** 0.25
DN_BETA = (8 * DEPTH) ** -0.25

IN_SPLITS = (
    A_HEADS * A_HEAD_DIM,
    A_KV_HEADS * A_HEAD_DIM,
    A_KV_HEADS * A_HEAD_DIM,
    IDX_HEADS * IDX_DIM,
    IDX_DIM,
    IDX_HEADS,
    B_Q_LORA,
    B_KV_LORA,
    B_ROPE,
    D_MODEL,
    D_MODEL,
)
IN_WIDTH = sum(IN_SPLITS)
IN_OFFSETS = tuple(sum(IN_SPLITS[: i + 1]) for i in range(len(IN_SPLITS) - 1))

kernel_name = "hybrid_dsa_mla_macaron_deepnorm"


def layer_norm(x, g, b, eps=1e-5):
    xf = x.astype(jnp.float32)
    mu = jnp.mean(xf, axis=-1, keepdims=True)
    var = jnp.mean(jnp.square(xf - mu), axis=-1, keepdims=True)
    y = (xf - mu) * lax.rsqrt(var + eps) * g.astype(jnp.float32) + b.astype(jnp.float32)
    return y.astype(x.dtype)


def rms_norm(x, g, eps=1e-6):
    xf = x.astype(jnp.float32)
    y = xf * lax.rsqrt(jnp.mean(jnp.square(xf), axis=-1, keepdims=True) + eps) * g.astype(jnp.float32)
    return y.astype(x.dtype)


def rope(x, pos):
    r = x.shape[-1]
    half = r // 2
    inv = ROPE_THETA ** (-(jnp.arange(half, dtype=jnp.float32) * 2.0 / r))
    ang = pos[:, None] * inv[None, :]
    cos = jnp.cos(ang)[:, None, :]
    sin = jnp.sin(ang)[:, None, :]
    xf = x.astype(jnp.float32)
    x1, x2 = xf[..., :half], xf[..., half:]
    out = jnp.concatenate([x1 * cos - x2 * sin, x2 * cos + x1 * sin], axis=-1)
    return out.astype(x.dtype)


def partial_rope(x, pos, rot):
    return jnp.concatenate([rope(x[..., :rot], pos), x[..., rot:]], axis=-1)


def swiglu(x, w13, w2):
    gte, up = jnp.split(x @ w13, 2, axis=-1)
    return (jax.nn.silu(gte) * up) @ w2


def to_blocks(t):
    b, lp = t.shape[:2]
    t = t.reshape(b, lp // BLOCK_Q, BLOCK_Q, *t.shape[2:])
    return jnp.moveaxis(t, 1, 0)


def from_blocks(t):
    t = jnp.moveaxis(t, 0, 1)
    return t.reshape(t.shape[0], -1, *t.shape[3:])


def dsa_sparse_attention(q, k, v, q_idx, k_idx, w_idx, topk):
    b, lp = q.shape[:2]
    grp = A_HEADS // A_KV_HEADS
    key_pos = jnp.arange(lp)
    k_idx_f = k_idx.astype(jnp.float32)

    def block(args):
        qb, qib, wb, start = args
        qpos = start + jnp.arange(BLOCK_Q)
        causal = key_pos[None, :] <= qpos[:, None]
        rel = jax.nn.relu(jnp.einsum('bqhd,bkd->bqhk', qib.astype(jnp.float32), k_idx_f) * (IDX_DIM ** -0.5))
        score = jnp.einsum('bqh,bqhk->bqk', wb.astype(jnp.float32) * (IDX_HEADS ** -0.5), rel)
        score = jnp.where(causal[None], score, NEG)
        _, sel = lax.top_k(score, topk)
        valid = sel <= qpos[None, :, None]
        k_sel = jax.vmap(lambda kb, ib: kb[ib])(k, sel)
        v_sel = jax.vmap(lambda vb, ib: vb[ib])(v, sel)
        qg = qb.reshape(b, BLOCK_Q, A_KV_HEADS, grp, A_HEAD_DIM)
        s = jnp.einsum('bqhgd,bqkhd->bqhgk', qg, k_sel).astype(jnp.float32) * (A_HEAD_DIM ** -0.5)
        s = jnp.where(valid[:, :, None, None, :], s, NEG)
        p = jax.nn.softmax(s, axis=-1).astype(v.dtype)
        o = jnp.einsum('bqhgk,bqkhd->bqhgd', p, v_sel)
        return o.reshape(b, BLOCK_Q, A_HEADS * A_HEAD_DIM)

    starts = jnp.arange(lp // BLOCK_Q, dtype=jnp.int32) * BLOCK_Q
    out = lax.map(block, (to_blocks(q), to_blocks(q_idx), to_blocks(w_idx), starts))
    return from_blocks(out)


def mla_attention(q_nope, q_rope, k_nope, k_rope, v):
    b, lp = q_nope.shape[:2]
    key_pos = jnp.arange(lp)
    scale = (B_NOPE + B_ROPE) ** -0.5

    def block(args):
        qn, qr, start = args
        qpos = start + jnp.arange(BLOCK_Q)
        causal = key_pos[None, :] <= qpos[:, None]
        s = (jnp.einsum('bqhd,bkhd->bhqk', qn, k_nope)
             + jnp.einsum('bqhd,bkd->bhqk', qr, k_rope)).astype(jnp.float32) * scale
        s = jnp.where(causal[None, None], s, NEG)
        p = jax.nn.softmax(s, axis=-1).astype(v.dtype)
        o = jnp.einsum('bhqk,bkhd->bqhd', p, v)
        return o.reshape(b, BLOCK_Q, B_HEADS * B_V)

    starts = jnp.arange(lp // BLOCK_Q, dtype=jnp.int32) * BLOCK_Q
    out = lax.map(block, (to_blocks(q_nope), to_blocks(q_rope), starts))
    return from_blocks(out)


def hybrid_layer(h, pos, topk, ln_g, ln_b, ffn1_w13, ffn1_w2, w_in, mla_q_norm, mla_kv_norm,
                 mla_w_uq, mla_w_ukv, w_branch_a, w_branch_b, w_out, ffn2_w13, ffn2_w2):
    b, lp, _ = h.shape
    h = layer_norm(DN_ALPHA * h + 0.5 * swiglu(h, ffn1_w13, ffn1_w2), ln_g[0], ln_b[0])

    z = h @ w_in
    (a_q, a_k, a_v, i_q, i_k, i_w, b_cq, b_ckv, b_kr, g_a, g_b) = jnp.split(z, IN_OFFSETS, axis=-1)

    a_q = partial_rope(a_q.reshape(b, lp, A_HEADS, A_HEAD_DIM), pos, A_ROT)
    a_k = partial_rope(a_k.reshape(b, lp, A_KV_HEADS, A_HEAD_DIM), pos, A_ROT)
    a_v = a_v.reshape(b, lp, A_KV_HEADS, A_HEAD_DIM)
    i_q = partial_rope(i_q.reshape(b, lp, IDX_HEADS, IDX_DIM), pos, IDX_ROT)
    i_k = partial_rope(i_k.reshape(b, lp, 1, IDX_DIM), pos, IDX_ROT)[:, :, 0]
    o_a = dsa_sparse_attention(a_q, a_k, a_v, i_q, i_k, i_w, topk)

    c_q = rms_norm(b_cq, mla_q_norm)
    q = (c_q @ mla_w_uq).reshape(b, lp, B_HEADS, B_NOPE + B_ROPE)
    q_nope, q_rope = q[..., :B_NOPE], rope(q[..., B_NOPE:], pos)
    c_kv = rms_norm(b_ckv, mla_kv_norm)
    kv = (c_kv @ mla_w_ukv).reshape(b, lp, B_HEADS, B_NOPE + B_V)
    k_nope, v = kv[..., :B_NOPE], kv[..., B_NOPE:]
    k_rope = rope(b_kr[:, :, None, :], pos)[:, :, 0]
    o_b = mla_attention(q_nope, q_rope, k_nope, k_rope, v)

    mixed = jax.nn.sigmoid(g_a) * (o_a @ w_branch_a) + jax.nn.sigmoid(g_b) * (o_b @ w_branch_b)
    h = layer_norm(DN_ALPHA * h + mixed @ w_out, ln_g[1], ln_b[1])

    h = layer_norm(DN_ALPHA * h + 0.5 * swiglu(h, ffn2_w13, ffn2_w2), ln_g[2], ln_b[2])
    return h


def setup_inputs(seed: int = 0) -> dict:
    key = jax.random.key(seed)
    ks = jax.random.split(key, 16)
    f32 = jnp.float32

    def nrm(k, shape, fan_in, scale=1.0):
        return jax.random.normal(k, shape, f32) * (fan_in ** -0.5) * scale

    return {
        "x": jax.random.normal(ks[0], (BATCH, SEQ, D_MODEL), f32),
        "meta_tokens": jax.random.normal(ks[1], (N_META, D_MODEL), f32),
        "ln_g": 1.0 + 0.02 * jax.random.normal(ks[2], (DEPTH, 3, D_MODEL), f32),
        "ln_b": 0.02 * jax.random.normal(ks[3], (DEPTH, 3, D_MODEL), f32),
        "ffn1_w13": nrm(ks[4], (DEPTH, D_MODEL, 2 * D_FF), D_MODEL),
        "ffn1_w2": nrm(ks[5], (DEPTH, D_FF, D_MODEL), D_FF, DN_BETA),
        "w_in": nrm(ks[6], (DEPTH, D_MODEL, IN_WIDTH), D_MODEL),
        "mla_q_norm": 1.0 + 0.02 * jax.random.normal(ks[7], (DEPTH, B_Q_LORA), f32),
        "mla_kv_norm": 1.0 + 0.02 * jax.random.normal(ks[8], (DEPTH, B_KV_LORA), f32),
        "mla_w_uq": nrm(ks[9], (DEPTH, B_Q_LORA, B_HEADS * (B_NOPE + B_ROPE)), B_Q_LORA),
        "mla_w_ukv": nrm(ks[10], (DEPTH, B_KV_LORA, B_HEADS * (B_NOPE + B_V)), B_KV_LORA),
        "w_branch_a": nrm(ks[11], (DEPTH, A_HEADS * A_HEAD_DIM, D_MODEL), A_HEADS * A_HEAD_DIM),
        "w_branch_b": nrm(ks[12], (DEPTH, B_HEADS * B_V, D_MODEL), B_HEADS * B_V),
        "w_out": nrm(ks[13], (DEPTH, D_MODEL, D_MODEL), D_MODEL, DN_BETA),
        "ffn2_w13": nrm(ks[14], (DEPTH, D_MODEL, 2 * D_FF), D_MODEL),
        "ffn2_w2": nrm(ks[15], (DEPTH, D_FF, D_MODEL), D_FF, DN_BETA),
    }


def reference(x, meta_tokens, ln_g, ln_b, ffn1_w13, ffn1_w2, w_in, mla_q_norm, mla_kv_norm,
              mla_w_uq, mla_w_ukv, w_branch_a, w_branch_b, w_out, ffn2_w13, ffn2_w2):
    b, s, d = x.shape
    total = s + N_META
    lp = -(-total // BLOCK_Q) * BLOCK_Q
    meta = jnp.broadcast_to(meta_tokens[None].astype(x.dtype), (b, N_META, d))
    h = jnp.concatenate([meta, x, jnp.zeros((b, lp - total, d), x.dtype)], axis=1)
    pos = jnp.arange(lp, dtype=jnp.float32)
    topk = min(TOPK_MAX, s // 4)
    for layer in range(DEPTH):
        h = hybrid_layer(h, pos, topk, ln_g[layer], ln_b[layer], ffn1_w13[layer], ffn1_w2[layer],
                         w_in[layer], mla_q_norm[layer], mla_kv_norm[layer], mla_w_uq[layer],
                         mla_w_ukv[layer], w_branch_a[layer], w_branch_b[layer], w_out[layer],
                         ffn2_w13[layer], ffn2_w2[layer])
    return h[:, N_META:N_META + s]
```

```python
import functools

import jax
import jax.numpy as jnp
from jax import lax
from jax.experimental import pallas as pl
from jax.experimental.pallas import tpu as pltpu

N_META = 16
ROPE_THETA = 500000.0
BLOCK_Q = 128
TOPK_MAX = 256
A_HEADS = 8
A_KV_HEADS = 2
A_HEAD_DIM = 64
A_ROT = A_HEAD_DIM // 4
IDX_HEADS = 8
IDX_DIM = 64
IDX_ROT = IDX_DIM // 4
B_HEADS = 8
B_NOPE = 64
B_ROPE = 32
B_V = 64
B_Q_LORA = 256
B_KV_LORA = 128

LANES = 128
SUBLANES = 8
VMEM_LIMIT = 56 * 1024 * 1024

F32 = jnp.float32
BF16 = jnp.bfloat16
NEG_INF = float("-inf")
M_INIT = -1e30
ALL_KEYS = -3e38
NT_DIMS = (((1,), (1,)), ((), ()))


def _pick_tile(n, candidates):
    for c in candidates:
        if n % c == 0:
            return c
    return n


def _layer_norm_rows(y, g, b):
    mu = jnp.mean(y, axis=-1, keepdims=True)
    d = y - mu
    var = jnp.mean(d * d, axis=-1, keepdims=True)
    return d * lax.rsqrt(var + 1e-5) * g + b


def _ffn_kernel(x_ref, w13_ref, w2_ref, g_ref, b_ref, o_ref, *, alpha, d_ff, chunk):
    x = x_ref[...]
    xb = x.astype(BF16)
    acc = jnp.zeros(x.shape, F32)
    for c in range(d_ff // chunk):
        gate = jnp.dot(xb, w13_ref[:, c * chunk:(c + 1) * chunk], preferred_element_type=F32)
        up = jnp.dot(xb, w13_ref[:, d_ff + c * chunk:d_ff + (c + 1) * chunk],
                     preferred_element_type=F32)
        act = (gate * jax.nn.sigmoid(gate) * up).astype(BF16)
        acc = acc + jnp.dot(act, w2_ref[c * chunk:(c + 1) * chunk, :], preferred_element_type=F32)
    y = alpha * x + 0.5 * acc
    o_ref[...] = _layer_norm_rows(y, g_ref[...], b_ref[...])


def _ffn_ln(h2d, w13, w2, g, b, alpha):
    t, d = h2d.shape
    d_ff = w2.shape[0]
    tm = _pick_tile(t, (512, 256, 128, 64, 32, 16, 8))
    chunk = _pick_tile(d_ff, (1408, 1024, 512, 256, 128))
    const = lambda i: (0, 0)
    return pl.pallas_call(
        functools.partial(_ffn_kernel, alpha=alpha, d_ff=d_ff, chunk=chunk),
        grid=(t // tm,),
        in_specs=[
            pl.BlockSpec((tm, d), lambda i: (i, 0)),
            pl.BlockSpec((d, 2 * d_ff), const, pipeline_mode=pl.Buffered(1)),
            pl.BlockSpec((d_ff, d), const, pipeline_mode=pl.Buffered(1)),
            pl.BlockSpec((1, d), const),
            pl.BlockSpec((1, d), const),
        ],
        out_specs=pl.BlockSpec((tm, d), lambda i: (i, 0)),
        out_shape=jax.ShapeDtypeStruct((t, d), F32),
        compiler_params=pltpu.CompilerParams(
            dimension_semantics=("parallel",), vmem_limit_bytes=VMEM_LIMIT),
        name="ffn_ln",
    )(h2d, w13, w2, g.reshape(1, d), b.reshape(1, d))


_C_AQ, _C_AQR = 0, 512
_C_AK, _C_AKR = 1024, 1280
_C_AV = 1536
_C_IQ, _C_IQR = 1664, 2176
_C_IK, _C_IKR = 2688, 2816
_C_IW = 2944
_C_CQ = 3072
_C_CKV = 3328
_C_KR, _C_KRR = 3456, 3584
_C_GA, _C_GB = 3712, 4736
_C_END = 5760


def _proj_kernel(h_ref, w_ref, cosa_ref, sina_ref, cosm_ref, sinm_ref, qn_ref, kvn_ref,
                 wuq_ref, wukv_ref,
                 aq_ref, akd_ref, av_ref, iq_ref, ikd_ref, iw_ref, qm_ref, km_ref, vm_ref,
                 sa_ref, sb_ref, *, mla_scale):
    hb = h_ref[0].astype(BF16)

    def proj(c0, c1):
        return jnp.dot(hb, w_ref[:, c0:c1], preferred_element_type=F32)

    cosa = cosa_ref[...]
    sina = sina_ref[...]
    cosa4 = jnp.concatenate([cosa] * 4, axis=1)
    sina4 = jnp.concatenate([sina] * 4, axis=1)
    qscale = A_HEAD_DIM ** -0.5
    aq = (proj(_C_AQ, _C_AQ + 512) * cosa4 + proj(_C_AQR, _C_AQR + 512) * sina4) * qscale
    aq_ref[0] = aq.astype(BF16)
    iscale = IDX_DIM ** -0.5
    iq = (proj(_C_IQ, _C_IQ + 512) * cosa4 + proj(_C_IQR, _C_IQR + 512) * sina4) * iscale
    iq_ref[0] = iq.astype(BF16)
    for g in range(A_KV_HEADS):
        ak = (proj(_C_AK + 128 * g, _C_AK + 128 * (g + 1)) * cosa
              + proj(_C_AKR + 128 * g, _C_AKR + 128 * (g + 1)) * sina)
        akd_ref[0, g] = ak.astype(BF16)
    av_ref[0] = proj(_C_AV, _C_AV + 128).astype(BF16)
    ik = proj(_C_IK, _C_IK + 128) * cosa + proj(_C_IKR, _C_IKR + 128) * sina
    ikd_ref[0] = ik.astype(BF16)
    iw_ref[0] = proj(_C_IW, _C_IW + 128) * (IDX_HEADS ** -0.5)
    sa_ref[0] = jax.nn.sigmoid(proj(_C_GA, _C_GA + 1024)).astype(BF16)
    sb_ref[0] = jax.nn.sigmoid(proj(_C_GB, _C_GB + 1024)).astype(BF16)

    cosm = cosm_ref[...]
    sinm = sinm_ref[...]
    cq = proj(_C_CQ, _C_CQ + B_Q_LORA)
    cq = cq * lax.rsqrt(jnp.mean(cq * cq, axis=-1, keepdims=True) + 1e-6) * qn_ref[...]
    cqb = cq.astype(BF16)
    for hd in range(B_HEADS):
        q = jnp.dot(cqb, wuq_ref[:, 128 * hd:128 * (hd + 1)], preferred_element_type=F32)
        qr = jnp.dot(cqb, wuq_ref[:, 1024 + 128 * hd:1024 + 128 * (hd + 1)],
                     preferred_element_type=F32)
        qm_ref[0, :, 128 * hd:128 * (hd + 1)] = ((q * cosm + qr * sinm) * mla_scale).astype(BF16)
    ckv = proj(_C_CKV, _C_CKV + B_KV_LORA)
    ckv = ckv * lax.rsqrt(jnp.mean(ckv * ckv, axis=-1, keepdims=True) + 1e-6) * kvn_ref[...]
    ckvb = ckv.astype(BF16)
    kr = proj(_C_KR, _C_KR + 128) * cosm + proj(_C_KRR, _C_KRR + 128) * sinm
    for hd in range(B_HEADS):
        kn = jnp.dot(ckvb, wukv_ref[:, 128 * hd:128 * (hd + 1)], preferred_element_type=F32)
        km_ref[0, hd] = (kn + kr).astype(BF16)
    vm_ref[0] = jnp.dot(ckvb, wukv_ref[:, 1024:1536], preferred_element_type=F32).astype(BF16)


def _rot_cols(w, heads, dim, rot):
    d_in = w.shape[0]
    w3 = w.reshape(d_in, heads, dim)
    half = rot // 2
    r = jnp.concatenate([-w3[:, :, half:rot], w3[:, :, :half],
                         jnp.zeros((d_in, heads, dim - rot), w.dtype)], axis=-1)
    return r.reshape(d_in, heads * dim)


def _prep_proj_weights(w_in, w_uq, w_ukv):
    d = w_in.shape[0]
    o = 0
    segs = {}
    for name, width in (("aq", 512), ("ak", 128), ("av", 128), ("iq", 512), ("ik", 64), ("iw", 8),
                        ("cq", 256), ("ckv", 128), ("kr", 32), ("ga", 1024), ("gb", 1024)):
        segs[name] = w_in[:, o:o + width]
        o += width
    z = lambda n: jnp.zeros((d, n), w_in.dtype)
    ak_r = _rot_cols(segs["ak"], A_KV_HEADS, A_HEAD_DIM, A_ROT)
    dup = lambda w, g: jnp.concatenate([w[:, 64 * g:64 * (g + 1)]] * 2, axis=1)
    ik_r = _rot_cols(segs["ik"], 1, IDX_DIM, IDX_ROT)
    kr_r = _rot_cols(segs["kr"], 1, B_ROPE, B_ROPE)
    cols = [
        segs["aq"], _rot_cols(segs["aq"], A_HEADS, A_HEAD_DIM, A_ROT),
        dup(segs["ak"], 0), dup(segs["ak"], 1), dup(ak_r, 0), dup(ak_r, 1),
        segs["av"],
        segs["iq"], _rot_cols(segs["iq"], IDX_HEADS, IDX_DIM, IDX_ROT),
        segs["ik"], segs["ik"], ik_r, ik_r,
        segs["iw"], z(120),
        segs["cq"], segs["ckv"],
        z(64), segs["kr"], z(32), z(64), kr_r, z(32),
        segs["ga"], segs["gb"],
    ]
    wcat = jnp.concatenate(cols, axis=1).astype(BF16)
    assert wcat.shape[1] == _C_END

    lq = w_uq.shape[0]
    uq3 = w_uq.reshape(lq, B_HEADS, B_NOPE + B_ROPE)
    uq_pad = jnp.concatenate([uq3, jnp.zeros((lq, B_HEADS, 32), w_uq.dtype)], axis=-1)
    rope_part = uq3[:, :, B_NOPE:]
    uq_rot = jnp.concatenate([jnp.zeros((lq, B_HEADS, B_NOPE), w_uq.dtype),
                              -rope_part[:, :, B_ROPE // 2:], rope_part[:, :, :B_ROPE // 2],
                              jnp.zeros((lq, B_HEADS, 32), w_uq.dtype)], axis=-1)
    wuq = jnp.concatenate([uq_pad.reshape(lq, 1024), uq_rot.reshape(lq, 1024)], axis=1).astype(BF16)

    lkv = w_ukv.shape[0]
    ukv3 = w_ukv.reshape(lkv, B_HEADS, B_NOPE + B_V)
    kn_pad = jnp.concatenate([ukv3[:, :, :B_NOPE], jnp.zeros((lkv, B_HEADS, 64), w_ukv.dtype)], axis=-1)
    wukv = jnp.concatenate([kn_pad.reshape(lkv, 1024), ukv3[:, :, B_NOPE:].reshape(lkv, 512)],
                           axis=1).astype(BF16)
    return wcat, wuq, wukv


def _rope_tables(lp):
    pos = jnp.arange(lp, dtype=F32)

    def cs(r):
        half = r // 2
        inv = ROPE_THETA ** (-(jnp.arange(half, dtype=F32) * 2.0 / r))
        ang = pos[:, None] * inv[None, :]
        return jnp.cos(ang), jnp.sin(ang)

    ca, sa = cs(A_ROT)
    one = lambda n: jnp.ones((lp, n), F32)
    zero = lambda n: jnp.zeros((lp, n), F32)
    cos64 = jnp.concatenate([ca, ca, one(A_HEAD_DIM - A_ROT)], axis=1)
    sin64 = jnp.concatenate([sa, sa, zero(A_HEAD_DIM - A_ROT)], axis=1)
    cosa = jnp.concatenate([cos64, cos64], axis=1)
    sina = jnp.concatenate([sin64, sin64], axis=1)
    cm, sm = cs(B_ROPE)
    cosm = jnp.concatenate([one(B_NOPE), cm, cm, one(32)], axis=1)
    sinm = jnp.concatenate([zero(B_NOPE), sm, sm, zero(32)], axis=1)
    return cosa, sina, cosm, sinm


def _projection(h3d, wcat, wuq, wukv, qn, kvn, tables):
    b, lp, d = h3d.shape
    tm = _pick_tile(lp, (640, 512, 256, 128, 64, 32, 16))
    cosa, sina, cosm, sinm = tables
    const2 = lambda bi, i: (0, 0)
    row3 = lambda bi, i: (bi, i, 0)
    tab = pl.BlockSpec((tm, 128), lambda bi, i: (i, 0))
    out_shapes = [
        jax.ShapeDtypeStruct((b, lp, 512), BF16),
        jax.ShapeDtypeStruct((b, A_KV_HEADS, lp, 128), BF16),
        jax.ShapeDtypeStruct((b, lp, 128), BF16),
        jax.ShapeDtypeStruct((b, lp, 512), BF16),
        jax.ShapeDtypeStruct((b, lp, 128), BF16),
        jax.ShapeDtypeStruct((b, lp, 128), F32),
        jax.ShapeDtypeStruct((b, lp, 1024), BF16),
        jax.ShapeDtypeStruct((b, B_HEADS, lp, 128), BF16),
        jax.ShapeDtypeStruct((b, lp, 512), BF16),
        jax.ShapeDtypeStruct((b, lp, 1024), BF16),
        jax.ShapeDtypeStruct((b, lp, 1024), BF16),
    ]
    out_specs = [
        pl.BlockSpec((1, tm, 512), row3),
        pl.BlockSpec((1, A_KV_HEADS, tm, 128), lambda bi, i: (bi, 0, i, 0)),
        pl.BlockSpec((1, tm, 128), row3),
        pl.BlockSpec((1, tm, 512), row3),
        pl.BlockSpec((1, tm, 128), row3),
        pl.BlockSpec((1, tm, 128), row3),
        pl.BlockSpec((1, tm, 1024), row3),
        pl.BlockSpec((1, B_HEADS, tm, 128), lambda bi, i: (bi, 0, i, 0)),
        pl.BlockSpec((1, tm, 512), row3),
        pl.BlockSpec((1, tm, 1024), row3),
        pl.BlockSpec((1, tm, 1024), row3),
    ]
    return pl.pallas_call(
        functools.partial(_proj_kernel, mla_scale=(B_NOPE + B_ROPE) ** -0.5),
        grid=(b, lp // tm),
        in_specs=[
            pl.BlockSpec((1, tm, d), row3),
            pl.BlockSpec(wcat.shape, const2, pipeline_mode=pl.Buffered(1)),
            tab, tab, tab, tab,
            pl.BlockSpec((1, B_Q_LORA), const2),
            pl.BlockSpec((1, B_KV_LORA), const2),
            pl.BlockSpec(wuq.shape, const2, pipeline_mode=pl.Buffered(1)),
            pl.BlockSpec(wukv.shape, const2, pipeline_mode=pl.Buffered(1)),
        ],
        out_specs=out_specs,
        out_shape=out_shapes,
        compiler_params=pltpu.CompilerParams(
            dimension_semantics=("parallel", "parallel"), vmem_limit_bytes=VMEM_LIMIT),
        name="in_proj",
    )(h3d, wcat, cosa, sina, cosm, sinm, qn.reshape(1, -1), kvn.reshape(1, -1), wuq, wukv)


def _chunk_loop(n128, body, init):
    n4 = n128 // 4
    carry = lax.fori_loop(0, n4, lambda j, c: body(pl.multiple_of(j * 512, 512), 512, c), init)
    return lax.fori_loop(n4 * 4, n128, lambda j, c: body(pl.multiple_of(j * 128, 128), 128, c), carry)


def _float_key(x):
    bits = lax.bitcast_convert_type(x, jnp.int32)
    return bits ^ ((bits >> 31) & 0x7FFFFFFF)


def _key_float(k):
    return lax.bitcast_convert_type(k ^ ((k >> 31) & 0x7FFFFFFF), F32)


def _finish_heads(acc_ref_g, n_heads, bq):
    outs = []
    for hh in range(n_heads):
        a = acc_ref_g[:, hh * bq:(hh + 1) * bq]
        o_t = a / a[B_V:B_V + 1, :]
        outs.append(jnp.transpose(o_t))
    return outs


def _pack_pairs(blocks):
    lane = lax.broadcasted_iota(jnp.int32, blocks[0].shape, 1)
    pairs = []
    for j in range(len(blocks) // 2):
        hi = pltpu.roll(blocks[2 * j + 1], 64, axis=1)
        pairs.append(jnp.where(lane < 64, blocks[2 * j], hi))
    return jnp.concatenate(pairs, axis=1) if len(pairs) > 1 else pairs[0]


def _attend_chunk(x, m_old, acc_ref_g, vt_blocks):
    m_new = jnp.maximum(m_old, jnp.max(x, axis=0, keepdims=True))
    p = jnp.exp(x - m_new).astype(BF16)
    alpha = jnp.exp(m_old - m_new)
    pv = None
    for r, vt in enumerate(vt_blocks):
        t = jnp.dot(vt, p[r * 128:(r + 1) * 128, :], preferred_element_type=F32)
        pv = t if pv is None else pv + t
    acc_ref_g[...] = acc_ref_g[...] * alpha + pv
    return m_new


def _dsa_kernel(iq_ref, iwt_ref, ikd_ref, aq_ref, akd_ref, avt_ref, o_ref,
                st_ref, qi_ref, qa_ref, acc_ref, *, bq, topk):
    i = pl.program_id(1)
    n128 = (i + 1) * (bq // 128)
    grp = A_HEADS // A_KV_HEADS

    lane = lax.broadcasted_iota(jnp.int32, (bq, 128), 1)
    for hd in range(IDX_HEADS):
        keep = (lane < 64) if hd % 2 == 0 else (lane >= 64)
        pair = iq_ref[0, :, 128 * (hd // 2):128 * (hd // 2 + 1)].astype(F32)
        qi_ref[hd * bq:(hd + 1) * bq, :] = jnp.where(keep, pair, 0.0).astype(BF16)
        pair = aq_ref[0, :, 128 * (hd // 2):128 * (hd // 2 + 1)].astype(F32)
        qa_ref[hd // grp, (hd % grp) * bq:(hd % grp + 1) * bq, :] = jnp.where(keep, pair, 0.0).astype(BF16)

    qpos = i * bq + lax.broadcasted_iota(jnp.int32, (1, bq), 1)

    iwt = iwt_ref[0]

    def score_body(start, w, carry):
        kc = ikd_ref[0, pl.ds(start, w), :]
        x = lax.dot_general(kc, qi_ref[...], NT_DIMS, preferred_element_type=F32)
        acc = jnp.zeros((w, bq), F32)
        for hd in range(IDX_HEADS):
            acc = acc + jnp.maximum(x[:, hd * bq:(hd + 1) * bq], 0.0) * iwt[hd:hd + 1, :]
        kpos = start + lax.broadcasted_iota(jnp.int32, (w, bq), 0)
        st_ref[pl.ds(start, w), :] = jnp.where(kpos <= qpos, acc, NEG_INF)
        return carry

    _chunk_loop(n128, score_body, 0)

    def reduce_rows(s, op, acc):
        for r in range(s.shape[0] // SUBLANES):
            acc = op(acc, s[r * SUBLANES:(r + 1) * SUBLANES, :])
        return acc

    def minmax_body(start, w, carry):
        mn, mx = carry
        s = st_ref[pl.ds(start, w), :]
        mx = reduce_rows(s, jnp.maximum, mx)
        mn = reduce_rows(jnp.where(s == NEG_INF, jnp.inf, s), jnp.minimum, mn)
        return mn, mx

    mn8, mx8 = _chunk_loop(n128, minmax_body,
                           (jnp.full((SUBLANES, bq), jnp.inf, F32), jnp.full((SUBLANES, bq), NEG_INF, F32)))
    row_min = jnp.min(mn8, axis=0, keepdims=True)
    row_max = jnp.max(mx8, axis=0, keepdims=True)

    def count_where(pred):
        def body(start, w, accs):
            s = st_ref[pl.ds(start, w), :]
            kpos0 = start + lax.broadcasted_iota(jnp.int32, (SUBLANES, bq), 0)
            accs = list(accs)
            for r in range(w // SUBLANES):
                hit = pred(s[r * SUBLANES:(r + 1) * SUBLANES, :], kpos0 + r * SUBLANES)
                accs[r % 4] = accs[r % 4] + jnp.where(hit, 1.0, 0.0)
            return tuple(accs)
        z = jnp.zeros((SUBLANES, bq), F32)
        a = _chunk_loop(n128, body, (z, z, z, z))
        return jnp.sum((a[0] + a[1]) + (a[2] + a[3]), axis=0, keepdims=True)

    kf = float(topk)
    n_causal = (qpos + 1).astype(F32)
    need = n_causal > kf
    all_key = _float_key(jnp.full((1, bq), ALL_KEYS, F32))
    lo0 = jnp.where(need, _float_key(row_min), all_key)
    hi0 = jnp.where(need, _float_key(row_max), all_key)

    def is_active(lo, hi, cnt):
        return (lo < hi) & (cnt != kf)

    def bisect_pass(state):
        lo, hi, cnt = state
        act = is_active(lo, hi, cnt)
        mid = (lo >> 1) + (hi >> 1) + (((lo & 1) + (hi & 1) + 1) >> 1)
        cb = jnp.broadcast_to(_key_float(mid), (SUBLANES, bq))
        c_mid = count_where(lambda s, kpos: s >= cb)
        ge = c_mid >= kf
        up = act & ge
        dn = act & jnp.logical_not(ge)
        return (jnp.where(up, mid, lo), jnp.where(dn, mid - 1, hi), jnp.where(up, c_mid, cnt))

    def bisect_group(state):
        for _ in range(4):
            state = bisect_pass(state)
        return state

    def bisect_cond(state):
        return jnp.max(is_active(*state).astype(jnp.int32)) > 0

    lo, hi, cnt = lax.while_loop(bisect_cond, bisect_group, (lo0, hi0, n_causal))
    c_row = _key_float(lo)
    c8 = jnp.broadcast_to(c_row, (SUBLANES, bq))
    tie = need & (cnt > kf)
    any_tie = jnp.max(tie.astype(jnp.int32)) > 0

    def write_bias(sel_fn):
        def body(start, w, carry):
            s = st_ref[pl.ds(start, w), :]
            kpos = start + lax.broadcasted_iota(jnp.int32, (w, bq), 0)
            st_ref[pl.ds(start, w), :] = jnp.where(sel_fn(s, kpos), 0.0, NEG_INF)
            return carry
        _chunk_loop(n128, body, 0)

    @pl.when(jnp.logical_not(any_tie))
    def _():
        write_bias(lambda s, kpos: s >= c_row)

    @pl.when(any_tie)
    def _():
        n_gt = count_where(lambda s, kpos: s > c8)
        want = kf - n_gt

        def idx_pass(_, state):
            jl, jh = state
            mid = (jl + jh) >> 1
            mid8 = jnp.broadcast_to(mid, (SUBLANES, bq))
            c_le = count_where(lambda s, kpos: (s == c8) & (kpos <= mid8))
            ok = c_le >= want
            return jnp.where(ok, jl, mid + 1), jnp.where(ok, mid, jh)

        jl0 = jnp.zeros((1, bq), jnp.int32)
        jh0 = jnp.full((1, bq), 1, jnp.int32) * (n128 * 128 - 1)
        n_idx_pass = max(1, (st_ref.shape[0] - 1).bit_length())
        jcut, _ = lax.fori_loop(0, n_idx_pass, idx_pass, (jl0, jh0))
        jcut = jnp.where(tie, jcut, jnp.int32(2 ** 30))
        write_bias(lambda s, kpos: (s > c_row) | ((s == c_row) & (kpos <= jcut)))

    acc_ref[...] = jnp.zeros(acc_ref.shape, F32)

    def att_body(start, w, ms):
        bias = st_ref[pl.ds(start, w), :]
        bias4 = jnp.concatenate([bias] * grp, axis=1)
        j0 = start // 128
        out = []
        for g in range(A_KV_HEADS):
            kc = akd_ref[0, g, pl.ds(start, w), :]
            x = lax.dot_general(kc, qa_ref[g], NT_DIMS, preferred_element_type=F32) + bias4
            vts = [avt_ref[0, g, j0 + r] for r in range(w // 128)]
            out.append(_attend_chunk(x, ms[g], acc_ref.at[g], vts))
        return tuple(out)

    m0 = jnp.full((1, grp * bq), M_INIT, F32)
    _chunk_loop(n128, att_body, (m0, m0))

    blocks = []
    for g in range(A_KV_HEADS):
        blocks += _finish_heads(acc_ref.at[g], grp, bq)
    o_ref[0] = _pack_pairs(blocks).astype(BF16)


def _vt_blocks(v, heads):
    b, lp, _ = v.shape
    v4 = v.reshape(b, lp, heads, 64)
    ext = jnp.concatenate([v4, jnp.ones((b, lp, heads, 1), v.dtype),
                           jnp.zeros((b, lp, heads, 63), v.dtype)], axis=-1)
    ext = ext.reshape(b, lp // 128, 128, heads, 128)
    return jnp.transpose(ext, (0, 3, 1, 4, 2))


def _dsa_attention(iq, iwt, ikd, aq, akd, avt, topk):
    b, lp, _ = iq.shape
    bq = BLOCK_Q
    nq = lp // bq
    grp = A_HEADS // A_KV_HEADS
    return pl.pallas_call(
        functools.partial(_dsa_kernel, bq=bq, topk=topk),
        grid=(b, nq),
        in_specs=[
            pl.BlockSpec((1, bq, 512), lambda bi, i: (bi, i, 0)),
            pl.BlockSpec((1, IDX_HEADS, bq), lambda bi, i: (bi, 0, i)),
            pl.BlockSpec((1, lp, 128), lambda bi, i: (bi, 0, 0)),
            pl.BlockSpec((1, bq, 512), lambda bi, i: (bi, i, 0)),
            pl.BlockSpec((1, A_KV_HEADS, lp, 128), lambda bi, i: (bi, 0, 0, 0)),
            pl.BlockSpec((1, A_KV_HEADS, lp // 128, 128, 128), lambda bi, i: (bi, 0, 0, 0, 0)),
        ],
        out_specs=pl.BlockSpec((1, bq, 512), lambda bi, i: (bi, i, 0)),
        out_shape=jax.ShapeDtypeStruct((b, lp, 512), BF16),
        scratch_shapes=[
            pltpu.VMEM((lp, bq), F32),
            pltpu.VMEM((IDX_HEADS * bq, 128), BF16),
            pltpu.VMEM((A_KV_HEADS, grp * bq, 128), BF16),
            pltpu.VMEM((A_KV_HEADS, 128, grp * bq), F32),
        ],
        compiler_params=pltpu.CompilerParams(
            dimension_semantics=("parallel", "arbitrary"), vmem_limit_bytes=VMEM_LIMIT),
        name="dsa_attention",
    )(iq, iwt, ikd, aq, akd, avt)


def _mla_kernel(q_ref, k_ref, vt_ref, o_ref, acc_ref, *, bq):
    i = pl.program_id(2)
    qpos = i * bq + lax.broadcasted_iota(jnp.int32, (1, bq), 1)
    acc_ref[...] = jnp.zeros(acc_ref.shape, F32)

    def make_body(masked):
        def body(start, w, ms):
            j0 = start // 128
            out = []
            for hh in range(2):
                kc = k_ref[0, hh, pl.ds(start, w), :]
                x = lax.dot_general(kc, q_ref[0, :, 128 * hh:128 * (hh + 1)], NT_DIMS,
                                    preferred_element_type=F32)
                if masked:
                    kpos = start + lax.broadcasted_iota(jnp.int32, (w, bq), 0)
                    x = jnp.where(kpos <= qpos, x, NEG_INF)
                vts = [vt_ref[0, hh, j0 + r] for r in range(w // 128)]
                out.append(_attend_chunk(x, ms[hh], acc_ref.at[hh], vts))
            return tuple(out)
        return body

    m0 = jnp.full((1, bq), M_INIT, F32)
    ms = _chunk_loop(i * (bq // 128), make_body(False), (m0, m0))
    ms = lax.fori_loop(i * (bq // 128), (i + 1) * (bq // 128),
                       lambda j, c: make_body(True)(pl.multiple_of(j * 128, 128), 128, c), ms)
    blocks = _finish_heads(acc_ref.at[0], 1, bq) + _finish_heads(acc_ref.at[1], 1, bq)
    o_ref[0] = _pack_pairs(blocks).astype(BF16)


def _mla_attention(qm, km, vmt):
    b, lp, _ = qm.shape
    bq = BLOCK_Q
    nq = lp // bq
    return pl.pallas_call(
        functools.partial(_mla_kernel, bq=bq),
        grid=(b, B_HEADS // 2, nq),
        in_specs=[
            pl.BlockSpec((1, bq, 256), lambda bi, p, i: (bi, i, p)),
            pl.BlockSpec((1, 2, lp, 128), lambda bi, p, i: (bi, p, 0, 0)),
            pl.BlockSpec((1, 2, lp // 128, 128, 128), lambda bi, p, i: (bi, p, 0, 0, 0)),
        ],
        out_specs=pl.BlockSpec((1, bq, 128), lambda bi, p, i: (bi, i, p)),
        out_shape=jax.ShapeDtypeStruct((b, lp, 512), BF16),
        scratch_shapes=[pltpu.VMEM((2, 128, bq), F32)],
        compiler_params=pltpu.CompilerParams(
            dimension_semantics=("parallel", "parallel", "arbitrary"), vmem_limit_bytes=VMEM_LIMIT),
        name="mla_attention",
    )(qm, km, vmt)


def _merge_kernel(h_ref, oa_ref, ob_ref, sa_ref, sb_ref, wa_ref, wb_ref, wo_ref, g_ref, b_ref, o_ref,
                  *, alpha):
    pa = jnp.dot(oa_ref[...], wa_ref[...], preferred_element_type=F32)
    pb = jnp.dot(ob_ref[...], wb_ref[...], preferred_element_type=F32)
    mixed = sa_ref[...].astype(F32) * pa + sb_ref[...].astype(F32) * pb
    y = alpha * h_ref[...] + jnp.dot(mixed.astype(BF16), wo_ref[...], preferred_element_type=F32)
    o_ref[...] = _layer_norm_rows(y, g_ref[...], b_ref[...])


def _merge_ln(h2d, oa, ob, sa, sb, wa, wb, wo, g, b, alpha):
    t, d = h2d.shape
    tm = _pick_tile(t, (512, 256, 128, 64, 32, 16, 8))
    row = lambda i: (i, 0)
    const = lambda i: (0, 0)
    return pl.pallas_call(
        functools.partial(_merge_kernel, alpha=alpha),
        grid=(t // tm,),
        in_specs=[
            pl.BlockSpec((tm, d), row),
            pl.BlockSpec((tm, oa.shape[1]), row),
            pl.BlockSpec((tm, ob.shape[1]), row),
            pl.BlockSpec((tm, d), row),
            pl.BlockSpec((tm, d), row),
            pl.BlockSpec(wa.shape, const),
            pl.BlockSpec(wb.shape, const),
            pl.BlockSpec(wo.shape, const),
            pl.BlockSpec((1, d), const),
            pl.BlockSpec((1, d), const),
        ],
        out_specs=pl.BlockSpec((tm, d), row),
        out_shape=jax.ShapeDtypeStruct((t, d), F32),
        compiler_params=pltpu.CompilerParams(
            dimension_semantics=("parallel",), vmem_limit_bytes=VMEM_LIMIT),
        name="merge_ln",
    )(h2d, oa, ob, sa, sb, wa, wb, wo, g.reshape(1, d), b.reshape(1, d))


def kernel(x, meta_tokens, ln_g, ln_b, ffn1_w13, ffn1_w2, w_in, mla_q_norm, mla_kv_norm, mla_w_uq,
           mla_w_ukv, w_branch_a, w_branch_b, w_out, ffn2_w13, ffn2_w2):
    b, s, d = x.shape
    depth = ln_g.shape[0]
    alpha = (2 * depth) ** 0.25
    total = s + N_META
    lp = -(-total // BLOCK_Q) * BLOCK_Q
    topk = min(TOPK_MAX, s // 4)
    meta = jnp.broadcast_to(meta_tokens[None].astype(x.dtype), (b, N_META, d))
    h = jnp.concatenate([meta, x, jnp.zeros((b, lp - total, d), x.dtype)], axis=1)
    tables = _rope_tables(lp)
    t = b * lp
    for l in range(depth):
        h2 = _ffn_ln(h.reshape(t, d), ffn1_w13[l].astype(BF16), ffn1_w2[l].astype(BF16),
                     ln_g[l, 0], ln_b[l, 0], alpha)
        wcat, wuq, wukv = _prep_proj_weights(w_in[l], mla_w_uq[l], mla_w_ukv[l])
        (aq, akd, av, iq, ikd, iw, qm, km, vm, sa, sb) = _projection(
            h2.reshape(b, lp, d), wcat, wuq, wukv, mla_q_norm[l], mla_kv_norm[l], tables)
        iwt = jnp.swapaxes(iw[:, :, :IDX_HEADS], 1, 2)
        o_a = _dsa_attention(iq, iwt, ikd, aq, akd, _vt_blocks(av, A_KV_HEADS), topk)
        o_b = _mla_attention(qm, km, _vt_blocks(vm, B_HEADS))
        h3 = _merge_ln(h2, o_a.reshape(t, -1), o_b.reshape(t, -1), sa.reshape(t, d), sb.reshape(t, d),
                       w_branch_a[l].astype(BF16), w_branch_b[l].astype(BF16), w_out[l].astype(BF16),
                       ln_g[l, 1], ln_b[l, 1], alpha)
        h = _ffn_ln(h3, ffn2_w13[l].astype(BF16), ffn2_w2[l].astype(BF16), ln_g[l, 2], ln_b[l, 2],
                    alpha).reshape(b, lp, d)
    return h[:, N_META:N_META + s]
```

```python
import functools

import jax
import jax.numpy as jnp
from jax import lax
from jax.experimental import pallas as pl
from jax.experimental.pallas import tpu as pltpu

N_META = 16
ROPE_THETA = 500000.0
BLOCK_Q = 128
TOPK_MAX = 256
A_HEADS = 8
A_KV_HEADS = 2
A_HEAD_DIM = 64
A_ROT = A_HEAD_DIM // 4
IDX_HEADS = 8
IDX_DIM = 64
IDX_ROT = IDX_DIM // 4
B_HEADS = 8
B_NOPE = 64
B_ROPE = 32
B_V = 64
B_Q_LORA = 256
B_KV_LORA = 128

LANES = 128
SUBLANES = 8
VMEM_LIMIT = 56 * 1024 * 1024

F32 = jnp.float32
BF16 = jnp.bfloat16
NEG_INF = float("-inf")
M_INIT = -1e30
ALL_KEYS = -3e38
LOG2E = 1.4426950408889634
NT_DIMS = (((1,), (1,)), ((), ()))


def _pick_tile(n, candidates):
    for c in candidates:
        if n % c == 0:
            return c
    return n


def _layer_norm_rows(y, g, b):
    mu = jnp.mean(y, axis=-1, keepdims=True)
    d = y - mu
    var = jnp.mean(d * d, axis=-1, keepdims=True)
    return d * lax.rsqrt(var + 1e-5) * g + b


def _ffn_kernel(x_ref, w13_ref, w2_ref, g_ref, b_ref, o_ref, *, alpha, d_ff, chunk):
    x = x_ref[...]
    xb = x.astype(BF16)
    acc = jnp.zeros(x.shape, F32)
    for c in range(d_ff // chunk):
        gate = jnp.dot(xb, w13_ref[:, c * chunk:(c + 1) * chunk], preferred_element_type=F32)
        up = jnp.dot(xb, w13_ref[:, d_ff + c * chunk:d_ff + (c + 1) * chunk],
                     preferred_element_type=F32)
        act = (gate * jax.nn.sigmoid(gate) * up).astype(BF16)
        acc = acc + jnp.dot(act, w2_ref[c * chunk:(c + 1) * chunk, :], preferred_element_type=F32)
    y = alpha * x + 0.5 * acc
    o_ref[...] = _layer_norm_rows(y, g_ref[...], b_ref[...])


def _ffn_ln(h2d, w13, w2, g, b, alpha):
    t, d = h2d.shape
    d_ff = w2.shape[0]
    tm = _pick_tile(t, (512, 256, 128, 64, 32, 16, 8))
    chunk = _pick_tile(d_ff, (1408, 1024, 512, 256, 128))
    const = lambda i: (0, 0)
    return pl.pallas_call(
        functools.partial(_ffn_kernel, alpha=alpha, d_ff=d_ff, chunk=chunk),
        grid=(t // tm,),
        in_specs=[
            pl.BlockSpec((tm, d), lambda i: (i, 0)),
            pl.BlockSpec((d, 2 * d_ff), const, pipeline_mode=pl.Buffered(1)),
            pl.BlockSpec((d_ff, d), const, pipeline_mode=pl.Buffered(1)),
            pl.BlockSpec((1, d), const),
            pl.BlockSpec((1, d), const),
        ],
        out_specs=pl.BlockSpec((tm, d), lambda i: (i, 0)),
        out_shape=jax.ShapeDtypeStruct((t, d), F32),
        compiler_params=pltpu.CompilerParams(
            dimension_semantics=("parallel",), vmem_limit_bytes=VMEM_LIMIT),
        name="ffn_ln",
    )(h2d, w13, w2, g.reshape(1, d), b.reshape(1, d))


_C_AQ, _C_AQR = 0, 512
_C_AK, _C_AKR = 1024, 1280
_C_AV = 1536
_C_IQ, _C_IQR = 1664, 2176
_C_IK, _C_IKR = 2688, 2816
_C_IW = 2944
_C_CQ = 3072
_C_CKV = 3328
_C_KR, _C_KRR = 3456, 3584
_C_GA, _C_GB = 3712, 4736
_C_END = 5760


def _proj_kernel(h_ref, w_ref, cosa_ref, sina_ref, cosm_ref, sinm_ref, qn_ref, kvn_ref,
                 wuq_ref, wukv_ref,
                 aq_ref, akd_ref, av_ref, iq_ref, ikd_ref, iw_ref, qm_ref, km_ref, vm_ref,
                 sa_ref, sb_ref, *, mla_scale):
    hb = h_ref[0].astype(BF16)

    def proj(c0, c1):
        return jnp.dot(hb, w_ref[:, c0:c1], preferred_element_type=F32)

    cosa = cosa_ref[...]
    sina = sina_ref[...]
    cosa4 = jnp.concatenate([cosa] * 4, axis=1)
    sina4 = jnp.concatenate([sina] * 4, axis=1)
    qscale = A_HEAD_DIM ** -0.5 * LOG2E
    aq = (proj(_C_AQ, _C_AQ + 512) * cosa4 + proj(_C_AQR, _C_AQR + 512) * sina4) * qscale
    aq_ref[0] = aq.astype(BF16)
    iscale = IDX_DIM ** -0.5
    iq = (proj(_C_IQ, _C_IQ + 512) * cosa4 + proj(_C_IQR, _C_IQR + 512) * sina4) * iscale
    iq_ref[0] = iq.astype(BF16)
    for g in range(A_KV_HEADS):
        ak = (proj(_C_AK + 128 * g, _C_AK + 128 * (g + 1)) * cosa
              + proj(_C_AKR + 128 * g, _C_AKR + 128 * (g + 1)) * sina)
        akd_ref[0, g] = ak.astype(BF16)
    av_ref[0] = proj(_C_AV, _C_AV + 128).astype(BF16)
    ik = proj(_C_IK, _C_IK + 128) * cosa + proj(_C_IKR, _C_IKR + 128) * sina
    ikd_ref[0] = ik.astype(BF16)
    iw_ref[0] = proj(_C_IW, _C_IW + 128) * (IDX_HEADS ** -0.5)
    sa_ref[0] = jax.nn.sigmoid(proj(_C_GA, _C_GA + 1024)).astype(BF16)
    sb_ref[0] = jax.nn.sigmoid(proj(_C_GB, _C_GB + 1024)).astype(BF16)

    cosm = cosm_ref[...]
    sinm = sinm_ref[...]
    cq = proj(_C_CQ, _C_CQ + B_Q_LORA)
    cq = cq * lax.rsqrt(jnp.mean(cq * cq, axis=-1, keepdims=True) + 1e-6) * qn_ref[...]
    cqb = cq.astype(BF16)
    for hd in range(B_HEADS):
        q = jnp.dot(cqb, wuq_ref[:, 128 * hd:128 * (hd + 1)], preferred_element_type=F32)
        qr = jnp.dot(cqb, wuq_ref[:, 1024 + 128 * hd:1024 + 128 * (hd + 1)],
                     preferred_element_type=F32)
        qm_ref[0, :, 128 * hd:128 * (hd + 1)] = ((q * cosm + qr * sinm) * mla_scale).astype(BF16)
    ckv = proj(_C_CKV, _C_CKV + B_KV_LORA)
    ckv = ckv * lax.rsqrt(jnp.mean(ckv * ckv, axis=-1, keepdims=True) + 1e-6) * kvn_ref[...]
    ckvb = ckv.astype(BF16)
    kr = proj(_C_KR, _C_KR + 128) * cosm + proj(_C_KRR, _C_KRR + 128) * sinm
    for hd in range(B_HEADS):
        kn = jnp.dot(ckvb, wukv_ref[:, 128 * hd:128 * (hd + 1)], preferred_element_type=F32)
        km_ref[0, hd] = (kn + kr).astype(BF16)
    vm_ref[0] = jnp.dot(ckvb, wukv_ref[:, 1024:1536], preferred_element_type=F32).astype(BF16)


def _rot_cols(w, heads, dim, rot):
    d_in = w.shape[0]
    w3 = w.reshape(d_in, heads, dim)
    half = rot // 2
    r = jnp.concatenate([-w3[:, :, half:rot], w3[:, :, :half],
                         jnp.zeros((d_in, heads, dim - rot), w.dtype)], axis=-1)
    return r.reshape(d_in, heads * dim)


def _prep_proj_weights(w_in, w_uq, w_ukv):
    d = w_in.shape[0]
    o = 0
    segs = {}
    for name, width in (("aq", 512), ("ak", 128), ("av", 128), ("iq", 512), ("ik", 64), ("iw", 8),
                        ("cq", 256), ("ckv", 128), ("kr", 32), ("ga", 1024), ("gb", 1024)):
        segs[name] = w_in[:, o:o + width]
        o += width
    z = lambda n: jnp.zeros((d, n), w_in.dtype)
    ak_r = _rot_cols(segs["ak"], A_KV_HEADS, A_HEAD_DIM, A_ROT)
    dup = lambda w, g: jnp.concatenate([w[:, 64 * g:64 * (g + 1)]] * 2, axis=1)
    ik_r = _rot_cols(segs["ik"], 1, IDX_DIM, IDX_ROT)
    kr_r = _rot_cols(segs["kr"], 1, B_ROPE, B_ROPE)
    cols = [
        segs["aq"], _rot_cols(segs["aq"], A_HEADS, A_HEAD_DIM, A_ROT),
        dup(segs["ak"], 0), dup(segs["ak"], 1), dup(ak_r, 0), dup(ak_r, 1),
        segs["av"],
        segs["iq"], _rot_cols(segs["iq"], IDX_HEADS, IDX_DIM, IDX_ROT),
        segs["ik"], segs["ik"], ik_r, ik_r,
        segs["iw"], z(120),
        segs["cq"], segs["ckv"],
        z(64), segs["kr"], z(32), z(64), kr_r, z(32),
        segs["ga"], segs["gb"],
    ]
    wcat = jnp.concatenate(cols, axis=1).astype(BF16)
    assert wcat.shape[1] == _C_END

    lq = w_uq.shape[0]
    uq3 = w_uq.reshape(lq, B_HEADS, B_NOPE + B_ROPE)
    uq_pad = jnp.concatenate([uq3, jnp.zeros((lq, B_HEADS, 32), w_uq.dtype)], axis=-1)
    rope_part = uq3[:, :, B_NOPE:]
    uq_rot = jnp.concatenate([jnp.zeros((lq, B_HEADS, B_NOPE), w_uq.dtype),
                              -rope_part[:, :, B_ROPE // 2:], rope_part[:, :, :B_ROPE // 2],
                              jnp.zeros((lq, B_HEADS, 32), w_uq.dtype)], axis=-1)
    wuq = jnp.concatenate([uq_pad.reshape(lq, 1024), uq_rot.reshape(lq, 1024)], axis=1).astype(BF16)

    lkv = w_ukv.shape[0]
    ukv3 = w_ukv.reshape(lkv, B_HEADS, B_NOPE + B_V)
    kn_pad = jnp.concatenate([ukv3[:, :, :B_NOPE], jnp.zeros((lkv, B_HEADS, 64), w_ukv.dtype)], axis=-1)
    wukv = jnp.concatenate([kn_pad.reshape(lkv, 1024), ukv3[:, :, B_NOPE:].reshape(lkv, 512)],
                           axis=1).astype(BF16)
    return wcat, wuq, wukv


def _rope_tables(lp):
    pos = jnp.arange(lp, dtype=F32)

    def cs(r):
        half = r // 2
        inv = ROPE_THETA ** (-(jnp.arange(half, dtype=F32) * 2.0 / r))
        ang = pos[:, None] * inv[None, :]
        return jnp.cos(ang), jnp.sin(ang)

    ca, sa = cs(A_ROT)
    one = lambda n: jnp.ones((lp, n), F32)
    zero = lambda n: jnp.zeros((lp, n), F32)
    cos64 = jnp.concatenate([ca, ca, one(A_HEAD_DIM - A_ROT)], axis=1)
    sin64 = jnp.concatenate([sa, sa, zero(A_HEAD_DIM - A_ROT)], axis=1)
    cosa = jnp.concatenate([cos64, cos64], axis=1)
    sina = jnp.concatenate([sin64, sin64], axis=1)
    cm, sm = cs(B_ROPE)
    cosm = jnp.concatenate([one(B_NOPE), cm, cm, one(32)], axis=1)
    sinm = jnp.concatenate([zero(B_NOPE), sm, sm, zero(32)], axis=1)
    return cosa, sina, cosm, sinm


def _projection(h3d, wcat, wuq, wukv, qn, kvn, tables):
    b, lp, d = h3d.shape
    tm = _pick_tile(lp, (640, 512, 256, 128, 64, 32, 16))
    cosa, sina, cosm, sinm = tables
    const2 = lambda bi, i: (0, 0)
    row3 = lambda bi, i: (bi, i, 0)
    tab = pl.BlockSpec((tm, 128), lambda bi, i: (i, 0))
    out_shapes = [
        jax.ShapeDtypeStruct((b, lp, 512), BF16),
        jax.ShapeDtypeStruct((b, A_KV_HEADS, lp, 128), BF16),
        jax.ShapeDtypeStruct((b, lp, 128), BF16),
        jax.ShapeDtypeStruct((b, lp, 512), BF16),
        jax.ShapeDtypeStruct((b, lp, 128), BF16),
        jax.ShapeDtypeStruct((b, lp, 128), F32),
        jax.ShapeDtypeStruct((b, lp, 1024), BF16),
        jax.ShapeDtypeStruct((b, B_HEADS, lp, 128), BF16),
        jax.ShapeDtypeStruct((b, lp, 512), BF16),
        jax.ShapeDtypeStruct((b, lp, 1024), BF16),
        jax.ShapeDtypeStruct((b, lp, 1024), BF16),
    ]
    out_specs = [
        pl.BlockSpec((1, tm, 512), row3),
        pl.BlockSpec((1, A_KV_HEADS, tm, 128), lambda bi, i: (bi, 0, i, 0)),
        pl.BlockSpec((1, tm, 128), row3),
        pl.BlockSpec((1, tm, 512), row3),
        pl.BlockSpec((1, tm, 128), row3),
        pl.BlockSpec((1, tm, 128), row3),
        pl.BlockSpec((1, tm, 1024), row3),
        pl.BlockSpec((1, B_HEADS, tm, 128), lambda bi, i: (bi, 0, i, 0)),
        pl.BlockSpec((1, tm, 512), row3),
        pl.BlockSpec((1, tm, 1024), row3),
        pl.BlockSpec((1, tm, 1024), row3),
    ]
    return pl.pallas_call(
        functools.partial(_proj_kernel, mla_scale=(B_NOPE + B_ROPE) ** -0.5 * LOG2E),
        grid=(b, lp // tm),
        in_specs=[
            pl.BlockSpec((1, tm, d), row3),
            pl.BlockSpec(wcat.shape, const2, pipeline_mode=pl.Buffered(1)),
            tab, tab, tab, tab,
            pl.BlockSpec((1, B_Q_LORA), const2),
            pl.BlockSpec((1, B_KV_LORA), const2),
            pl.BlockSpec(wuq.shape, const2, pipeline_mode=pl.Buffered(1)),
            pl.BlockSpec(wukv.shape, const2, pipeline_mode=pl.Buffered(1)),
        ],
        out_specs=out_specs,
        out_shape=out_shapes,
        compiler_params=pltpu.CompilerParams(
            dimension_semantics=("parallel", "parallel"), vmem_limit_bytes=VMEM_LIMIT),
        name="in_proj",
    )(h3d, wcat, cosa, sina, cosm, sinm, qn.reshape(1, -1), kvn.reshape(1, -1), wuq, wukv)


def _float_key(x):
    bits = lax.bitcast_convert_type(x, jnp.int32)
    return bits ^ ((bits >> 31) & 0x7FFFFFFF)


def _key_float(k):
    return lax.bitcast_convert_type(k ^ ((k >> 31) & 0x7FFFFFFF), F32)


def _finish_heads(acc_ref_g, n_heads, bq):
    outs = []
    for hh in range(n_heads):
        a = acc_ref_g[:, hh * bq:(hh + 1) * bq]
        o_t = a / a[B_V:B_V + 1, :]
        outs.append(jnp.transpose(o_t))
    return outs


def _pack_pairs(blocks):
    lane = lax.broadcasted_iota(jnp.int32, blocks[0].shape, 1)
    pairs = []
    for j in range(len(blocks) // 2):
        hi = pltpu.roll(blocks[2 * j + 1], 64, axis=1)
        pairs.append(jnp.where(lane < 64, blocks[2 * j], hi))
    return jnp.concatenate(pairs, axis=1) if len(pairs) > 1 else pairs[0]


def _attention_steps(n_steps, scores_fn, vt_fn, acc_ref, buf_refs, n_cols):
    ng = acc_ref.shape[0]

    def stage_a(j, buf):
        mx = []
        for g, x in enumerate(scores_fn(j)):
            buf[g] = x
            mx.append(jnp.max(x, axis=0, keepdims=True))
        return tuple(mx)

    def stage_b(j, buf, ms, mx):
        out = []
        for g in range(ng):
            m_new = jnp.maximum(ms[g], mx[g])
            p = jnp.exp2(buf[g] - m_new).astype(BF16)
            alpha = jnp.exp2(ms[g] - m_new)
            acc_ref[g] = acc_ref[g] * alpha + jnp.dot(vt_fn(j, g), p, preferred_element_type=F32)
            out.append(m_new)
        return tuple(out)

    def pair(t, carry):
        ms, mx = carry
        j = 2 * t
        mx1 = stage_a(j + 1, buf_refs[1])
        ms = stage_b(j, buf_refs[0], ms, mx)
        mx2 = stage_a(j + 2, buf_refs[0])
        ms = stage_b(j + 1, buf_refs[1], ms, mx1)
        return ms, mx2

    m0 = jnp.full((1, n_cols), M_INIT, F32)
    n_pairs = (n_steps - 1) // 2
    ms, mx = lax.fori_loop(0, n_pairs, pair, ((m0,) * ng, stage_a(0, buf_refs[0])))
    j = 2 * n_pairs
    one_more = j + 1 < n_steps

    @pl.when(one_more)
    def _():
        mx1 = stage_a(j + 1, buf_refs[1])
        ms1 = stage_b(j, buf_refs[0], ms, mx)
        stage_b(j + 1, buf_refs[1], ms1, mx1)

    @pl.when(jnp.logical_not(one_more))
    def _():
        stage_b(j, buf_refs[0], ms, mx)


def _dsa_kernel(iq_ref, iwt_ref, ikd_ref, aq_ref, akd_ref, avt_ref, o_ref,
                st_ref, qi_ref, qa_ref, acc_ref, c_ref, xa_ref, xb_ref, *, bq, w, topk):
    i = pl.program_id(1)
    n_steps = ((i + 1) * bq + (w - 1)) // w
    grp = A_HEADS // A_KV_HEADS

    def key_steps(body, init):
        return lax.fori_loop(0, n_steps, lambda j, c: body(pl.multiple_of(j * w, w), j, c), init)

    lane = lax.broadcasted_iota(jnp.int32, (bq, 128), 1)
    for hd in range(IDX_HEADS):
        keep = (lane < 64) if hd % 2 == 0 else (lane >= 64)
        pair = iq_ref[0, :, 128 * (hd // 2):128 * (hd // 2 + 1)].astype(F32)
        qi_ref[hd * bq:(hd + 1) * bq, :] = jnp.where(keep, pair, 0.0).astype(BF16)
        pair = aq_ref[0, :, 128 * (hd // 2):128 * (hd // 2 + 1)].astype(F32)
        qa_ref[hd // grp, (hd % grp) * bq:(hd % grp + 1) * bq, :] = jnp.where(keep, pair, 0.0).astype(BF16)

    qpos = i * bq + lax.broadcasted_iota(jnp.int32, (1, bq), 1)

    iwt = iwt_ref[0]

    def score_body(start, j, carry):
        kc = ikd_ref[0, pl.ds(start, w), :]
        x = lax.dot_general(kc, qi_ref[...], NT_DIMS, preferred_element_type=F32)
        acc = jnp.zeros((w, bq), F32)
        for hd in range(IDX_HEADS):
            acc = acc + jnp.maximum(x[:, hd * bq:(hd + 1) * bq], 0.0) * iwt[hd:hd + 1, :]
        kpos = start + lax.broadcasted_iota(jnp.int32, (w, bq), 0)
        st_ref[pl.ds(start, w), :] = jnp.where(kpos <= qpos, acc, NEG_INF)
        return carry

    key_steps(score_body, 0)

    def sweep(fn, init):
        def body(start, j, accs):
            s = st_ref[pl.ds(start, w), :]
            kpos0 = start + lax.broadcasted_iota(jnp.int32, (SUBLANES, bq), 0)
            accs = list(accs)
            for r in range(w // SUBLANES):
                lane_acc = fn(s[r * SUBLANES:(r + 1) * SUBLANES, :], kpos0 + r * SUBLANES, accs[r % 4])
                accs[r % 4] = lane_acc
            return tuple(accs)
        return key_steps(body, (init,) * 4)

    def count_where(pred):
        a = sweep(lambda s, kpos, acc: acc + jnp.where(pred(s, kpos), 1.0, 0.0), jnp.zeros((SUBLANES, bq), F32))
        return jnp.sum((a[0] + a[1]) + (a[2] + a[3]), axis=0, keepdims=True)

    def min_where(pred):
        a = sweep(lambda s, kpos, acc: jnp.minimum(acc, jnp.where(pred(s, kpos), s, jnp.inf)),
                  jnp.full((SUBLANES, bq), jnp.inf, F32))
        return jnp.min(jnp.minimum(jnp.minimum(a[0], a[1]), jnp.minimum(a[2], a[3])), axis=0, keepdims=True)

    def rows8(v):
        return jnp.broadcast_to(v, (SUBLANES, bq))

    a = sweep(lambda s, kpos, acc: jnp.maximum(acc, s), jnp.full((SUBLANES, bq), NEG_INF, F32))
    row_max = jnp.max(jnp.maximum(jnp.maximum(a[0], a[1]), jnp.maximum(a[2], a[3])), axis=0, keepdims=True)
    row_min = min_where(lambda s, kpos: s != NEG_INF)

    kf = float(topk)
    n_causal = (qpos + 1).astype(F32)
    need = n_causal > kf
    all_key = _float_key(jnp.full((1, bq), ALL_KEYS, F32))
    lo0 = jnp.where(need, _float_key(row_min), all_key)
    hi0 = jnp.where(need, _float_key(row_max), all_key)

    def is_active(lo, hi, cnt):
        return (lo < hi) & (cnt > kf + 1.0)

    def bracket_pass(state, use_key_mid):
        lo, hi, cnt = state
        act = is_active(lo, hi, cnt)
        if use_key_mid:
            mid = (lo >> 1) + (hi >> 1) + (((lo & 1) + (hi & 1) + 1) >> 1)
        else:
            fmid = 0.5 * _key_float(lo) + 0.5 * _key_float(hi)
            mid = jnp.minimum(jnp.maximum(_float_key(fmid), lo + 1), hi)
        cb = rows8(_key_float(mid))
        c_mid = count_where(lambda s, kpos: s >= cb)
        ge = c_mid >= kf
        up = act & ge
        dn = act & jnp.logical_not(ge)
        return (jnp.where(up, mid, lo), jnp.where(dn, mid - 1, hi), jnp.where(up, c_mid, cnt))

    def bracket_group(state):
        for p in range(4):
            state = bracket_pass(state, use_key_mid=(p == 3))
        return state

    def bracket_cond(state):
        return jnp.max(is_active(*state).astype(jnp.int32)) > 0

    lo, hi, cnt = lax.while_loop(bracket_cond, bracket_group, (lo0, hi0, n_causal))
    c_lo = _key_float(lo)
    c_lo8 = rows8(c_lo)
    m_row = min_where(lambda s, kpos: s >= c_lo8)
    km = _float_key(m_row)
    k_above = jnp.where((km >= 0) & (km < 0x00800000), 0x00800000, km + 1)
    c_sel = jnp.where(need & (cnt > kf), _key_float(k_above), c_lo)
    c_sel8 = rows8(c_sel)
    n_sel = count_where(lambda s, kpos: s >= c_sel8)
    tie = need & (n_sel != kf)
    c_ref[...] = c_sel

    @pl.when(jnp.max(tie.astype(jnp.int32)) > 0)
    def _():
        want = kf - n_sel
        m8 = rows8(m_row)

        def idx_pass(_, state):
            jl, jh = state
            mid8 = rows8((jl + jh) >> 1)
            c_le = count_where(lambda s, kpos: (s == m8) & (kpos <= mid8))
            ok = c_le >= want
            return jnp.where(ok, jl, ((jl + jh) >> 1) + 1), jnp.where(ok, (jl + jh) >> 1, jh)

        jl0 = jnp.zeros((1, bq), jnp.int32)
        jh0 = jnp.zeros((1, bq), jnp.int32) + (n_steps * w - 1)
        n_idx_pass = max(1, (st_ref.shape[0] - 1).bit_length())
        jcut, _ = lax.fori_loop(0, n_idx_pass, idx_pass, (jl0, jh0))
        jcut8 = rows8(jnp.where(tie, jcut, jnp.int32(2 ** 30)))
        tie8 = rows8(tie)

        def drop_body(start, j, carry):
            for r in range(w // SUBLANES):
                s = st_ref[pl.ds(start + r * SUBLANES, SUBLANES), :]
                kpos = start + r * SUBLANES + lax.broadcasted_iota(jnp.int32, (SUBLANES, bq), 0)
                drop = tie8 & (s == m8) & (kpos > jcut8)
                st_ref[pl.ds(start + r * SUBLANES, SUBLANES), :] = jnp.where(drop, NEG_INF, s)
            return carry

        key_steps(drop_body, 0)
        c_ref[...] = jnp.where(tie, m_row, c_sel)

    acc_ref[...] = jnp.zeros(acc_ref.shape, F32)
    c_fin = c_ref[...]

    def att_scores(j):
        start = pl.multiple_of(j * w, w)
        bias = jnp.where(st_ref[pl.ds(start, w), :] >= c_fin, 0.0, NEG_INF)
        bias4 = jnp.concatenate([bias] * grp, axis=1)
        return [lax.dot_general(akd_ref[0, g, pl.ds(start, w), :], qa_ref[g], NT_DIMS,
                                preferred_element_type=F32) + bias4 for g in range(A_KV_HEADS)]

    _attention_steps(n_steps, att_scores, lambda j, g: avt_ref[0, g, j], acc_ref, (xa_ref, xb_ref), grp * bq)

    blocks = []
    for g in range(A_KV_HEADS):
        blocks += _finish_heads(acc_ref.at[g], grp, bq)
    o_ref[0] = _pack_pairs(blocks).astype(BF16)


def _vt_blocks(v, heads, blk):
    b, lp, _ = v.shape
    v4 = v.reshape(b, lp, heads, 64)
    ext = jnp.concatenate([v4, jnp.ones((b, lp, heads, 1), v.dtype),
                           jnp.zeros((b, lp, heads, 63), v.dtype)], axis=-1)
    ext = ext.reshape(b, lp // blk, blk, heads, 128)
    return jnp.transpose(ext, (0, 3, 1, 4, 2))


def _dsa_attention(iq, iwt, ikd, aq, akd, av, topk):
    b, lp, _ = iq.shape
    bq = BLOCK_Q
    nq = lp // bq
    w = _pick_tile(lp, (640, 512, 384, 256, 128))
    avt = _vt_blocks(av, A_KV_HEADS, w)
    grp = A_HEADS // A_KV_HEADS
    return pl.pallas_call(
        functools.partial(_dsa_kernel, bq=bq, w=w, topk=topk),
        grid=(b, nq),
        in_specs=[
            pl.BlockSpec((1, bq, 512), lambda bi, i: (bi, i, 0)),
            pl.BlockSpec((1, IDX_HEADS, bq), lambda bi, i: (bi, 0, i)),
            pl.BlockSpec((1, lp, 128), lambda bi, i: (bi, 0, 0)),
            pl.BlockSpec((1, bq, 512), lambda bi, i: (bi, i, 0)),
            pl.BlockSpec((1, A_KV_HEADS, lp, 128), lambda bi, i: (bi, 0, 0, 0)),
            pl.BlockSpec((1, A_KV_HEADS, lp // w, 128, w), lambda bi, i: (bi, 0, 0, 0, 0)),
        ],
        out_specs=pl.BlockSpec((1, bq, 512), lambda bi, i: (bi, i, 0)),
        out_shape=jax.ShapeDtypeStruct((b, lp, 512), BF16),
        scratch_shapes=[
            pltpu.VMEM((lp, bq), F32),
            pltpu.VMEM((IDX_HEADS * bq, 128), BF16),
            pltpu.VMEM((A_KV_HEADS, grp * bq, 128), BF16),
            pltpu.VMEM((A_KV_HEADS, 128, grp * bq), F32),
            pltpu.VMEM((1, bq), F32),
            pltpu.VMEM((A_KV_HEADS, w, grp * bq), F32),
            pltpu.VMEM((A_KV_HEADS, w, grp * bq), F32),
        ],
        compiler_params=pltpu.CompilerParams(
            dimension_semantics=("parallel", "arbitrary"), vmem_limit_bytes=VMEM_LIMIT),
        name="dsa_attention",
    )(iq, iwt, ikd, aq, akd, avt)


def _mla_kernel(q_ref, k_ref, vt_ref, o_ref, acc_ref, tri_ref, xa_ref, xb_ref, *, bq):
    i = pl.program_id(2)
    acc_ref[...] = jnp.zeros(acc_ref.shape, F32)
    kpos = lax.broadcasted_iota(jnp.int32, (bq, bq), 0)
    qpos = lax.broadcasted_iota(jnp.int32, (bq, bq), 1)
    tri_ref[0] = jnp.zeros((bq, bq), F32)
    tri_ref[1] = jnp.where(kpos <= qpos, 0.0, NEG_INF)

    def scores(j):
        start = pl.multiple_of(j * bq, bq)
        bias = tri_ref[(j == i).astype(jnp.int32)]
        return [lax.dot_general(k_ref[0, hh, pl.ds(start, bq), :], q_ref[0, :, 128 * hh:128 * (hh + 1)],
                                NT_DIMS, preferred_element_type=F32) + bias for hh in range(2)]

    _attention_steps(i + 1, scores, lambda j, hh: vt_ref[0, hh, j], acc_ref, (xa_ref, xb_ref), bq)
    blocks = _finish_heads(acc_ref.at[0], 1, bq) + _finish_heads(acc_ref.at[1], 1, bq)
    o_ref[0] = _pack_pairs(blocks).astype(BF16)


def _mla_attention(qm, km, vm):
    b, lp, _ = qm.shape
    bq = _pick_tile(lp, (640, 512, 256, 128))
    nq = lp // bq
    vmt = _vt_blocks(vm, B_HEADS, bq)
    return pl.pallas_call(
        functools.partial(_mla_kernel, bq=bq),
        grid=(b, B_HEADS // 2, nq),
        in_specs=[
            pl.BlockSpec((1, bq, 256), lambda bi, p, i: (bi, i, p)),
            pl.BlockSpec((1, 2, lp, 128), lambda bi, p, i: (bi, p, 0, 0)),
            pl.BlockSpec((1, 2, nq, 128, bq), lambda bi, p, i: (bi, p, 0, 0, 0)),
        ],
        out_specs=pl.BlockSpec((1, bq, 128), lambda bi, p, i: (bi, i, p)),
        out_shape=jax.ShapeDtypeStruct((b, lp, 512), BF16),
        scratch_shapes=[pltpu.VMEM((2, 128, bq), F32), pltpu.VMEM((2, bq, bq), F32),
                        pltpu.VMEM((2, bq, bq), F32), pltpu.VMEM((2, bq, bq), F32)],
        compiler_params=pltpu.CompilerParams(
            dimension_semantics=("parallel", "parallel", "arbitrary"), vmem_limit_bytes=VMEM_LIMIT),
        name="mla_attention",
    )(qm, km, vmt)


def _merge_kernel(h_ref, oa_ref, ob_ref, sa_ref, sb_ref, wa_ref, wb_ref, wo_ref, g_ref, b_ref, o_ref,
                  *, alpha):
    pa = jnp.dot(oa_ref[...], wa_ref[...], preferred_element_type=F32)
    pb = jnp.dot(ob_ref[...], wb_ref[...], preferred_element_type=F32)
    mixed = sa_ref[...].astype(F32) * pa + sb_ref[...].astype(F32) * pb
    y = alpha * h_ref[...] + jnp.dot(mixed.astype(BF16), wo_ref[...], preferred_element_type=F32)
    o_ref[...] = _layer_norm_rows(y, g_ref[...], b_ref[...])


def _merge_ln(h2d, oa, ob, sa, sb, wa, wb, wo, g, b, alpha):
    t, d = h2d.shape
    tm = _pick_tile(t, (512, 256, 128, 64, 32, 16, 8))
    row = lambda i: (i, 0)
    const = lambda i: (0, 0)
    return pl.pallas_call(
        functools.partial(_merge_kernel, alpha=alpha),
        grid=(t // tm,),
        in_specs=[
            pl.BlockSpec((tm, d), row),
            pl.BlockSpec((tm, oa.shape[1]), row),
            pl.BlockSpec((tm, ob.shape[1]), row),
            pl.BlockSpec((tm, d), row),
            pl.BlockSpec((tm, d), row),
            pl.BlockSpec(wa.shape, const),
            pl.BlockSpec(wb.shape, const),
            pl.BlockSpec(wo.shape, const),
            pl.BlockSpec((1, d), const),
            pl.BlockSpec((1, d), const),
        ],
        out_specs=pl.BlockSpec((tm, d), row),
        out_shape=jax.ShapeDtypeStruct((t, d), F32),
        compiler_params=pltpu.CompilerParams(
            dimension_semantics=("parallel",), vmem_limit_bytes=VMEM_LIMIT),
        name="merge_ln",
    )(h2d, oa, ob, sa, sb, wa, wb, wo, g.reshape(1, d), b.reshape(1, d))


def kernel(x, meta_tokens, ln_g, ln_b, ffn1_w13, ffn1_w2, w_in, mla_q_norm, mla_kv_norm, mla_w_uq,
           mla_w_ukv, w_branch_a, w_branch_b, w_out, ffn2_w13, ffn2_w2):
    b, s, d = x.shape
    depth = ln_g.shape[0]
    alpha = (2 * depth) ** 0.25
    total = s + N_META
    lp = -(-total // BLOCK_Q) * BLOCK_Q
    topk = min(TOPK_MAX, s // 4)
    meta = jnp.broadcast_to(meta_tokens[None].astype(x.dtype), (b, N_META, d))
    h = jnp.concatenate([meta, x, jnp.zeros((b, lp - total, d), x.dtype)], axis=1)
    tables = _rope_tables(lp)
    t = b * lp
    for l in range(depth):
        h2 = _ffn_ln(h.reshape(t, d), ffn1_w13[l].astype(BF16), ffn1_w2[l].astype(BF16),
                     ln_g[l, 0], ln_b[l, 0], alpha)
        wcat, wuq, wukv = _prep_proj_weights(w_in[l], mla_w_uq[l], mla_w_ukv[l])
        (aq, akd, av, iq, ikd, iw, qm, km, vm, sa, sb) = _projection(
            h2.reshape(b, lp, d), wcat, wuq, wukv, mla_q_norm[l], mla_kv_norm[l], tables)
        iwt = jnp.swapaxes(iw[:, :, :IDX_HEADS], 1, 2)
        o_a = _dsa_attention(iq, iwt, ikd, aq, akd, av, topk)
        o_b = _mla_attention(qm, km, vm)
        h3 = _merge_ln(h2, o_a.reshape(t, -1), o_b.reshape(t, -1), sa.reshape(t, d), sb.reshape(t, d),
                       w_branch_a[l].astype(BF16), w_branch_b[l].astype(BF16), w_out[l].astype(BF16),
                       ln_g[l, 1], ln_b[l, 1], alpha)
        h = _ffn_ln(h3, ffn2_w13[l].astype(BF16), ffn2_w2[l].astype(BF16), ln_g[l, 2], ln_b[l, 2],
                    alpha).reshape(b, lp, d)
    return h[:, N_META:N_META + s]
```

```python
import functools

import jax
import jax.numpy as jnp
from jax import lax
from jax.experimental import pallas as pl
from jax.experimental.pallas import tpu as pltpu

N_META = 16
ROPE_THETA = 500000.0
BLOCK_Q = 128
TOPK_MAX = 256
A_HEADS = 8
A_KV_HEADS = 2
A_HEAD_DIM = 64
A_ROT = A_HEAD_DIM // 4
IDX_HEADS = 8
IDX_DIM = 64
IDX_ROT = IDX_DIM // 4
B_HEADS = 8
B_NOPE = 64
B_ROPE = 32
B_V = 64
B_Q_LORA = 256
B_KV_LORA = 128

LANES = 128
SUBLANES = 8
VMEM_LIMIT = 56 * 1024 * 1024

F32 = jnp.float32
BF16 = jnp.bfloat16
NEG_INF = float("-inf")
M_INIT = -1e30
ALL_KEYS = -3e38
LOG2E = 1.4426950408889634
NT_DIMS = (((1,), (1,)), ((), ()))


def _pick_tile(n, candidates):
    for c in candidates:
        if n % c == 0:
            return c
    return n


def _layer_norm_rows(y, g, b):
    mu = jnp.mean(y, axis=-1, keepdims=True)
    d = y - mu
    var = jnp.mean(d * d, axis=-1, keepdims=True)
    return d * lax.rsqrt(var + 1e-5) * g + b


def _ffn_kernel(x_ref, w13_ref, w2_ref, g_ref, b_ref, o_ref, *, alpha, d_ff, chunk):
    x = x_ref[...]
    xb = x.astype(BF16)
    acc = jnp.zeros(x.shape, F32)
    for c in range(d_ff // chunk):
        gate = jnp.dot(xb, w13_ref[:, c * chunk:(c + 1) * chunk], preferred_element_type=F32)
        up = jnp.dot(xb, w13_ref[:, d_ff + c * chunk:d_ff + (c + 1) * chunk],
                     preferred_element_type=F32)
        act = (gate * jax.nn.sigmoid(gate) * up).astype(BF16)
        acc = acc + jnp.dot(act, w2_ref[c * chunk:(c + 1) * chunk, :], preferred_element_type=F32)
    y = alpha * x + 0.5 * acc
    o_ref[...] = _layer_norm_rows(y, g_ref[...], b_ref[...])


def _ffn_ln(h2d, w13, w2, g, b, alpha):
    t, d = h2d.shape
    d_ff = w2.shape[0]
    tm = _pick_tile(t, (512, 256, 128, 64, 32, 16, 8))
    chunk = _pick_tile(d_ff, (1408, 1024, 512, 256, 128))
    const = lambda i: (0, 0)
    return pl.pallas_call(
        functools.partial(_ffn_kernel, alpha=alpha, d_ff=d_ff, chunk=chunk),
        grid=(t // tm,),
        in_specs=[
            pl.BlockSpec((tm, d), lambda i: (i, 0)),
            pl.BlockSpec((d, 2 * d_ff), const, pipeline_mode=pl.Buffered(1)),
            pl.BlockSpec((d_ff, d), const, pipeline_mode=pl.Buffered(1)),
            pl.BlockSpec((1, d), const),
            pl.BlockSpec((1, d), const),
        ],
        out_specs=pl.BlockSpec((tm, d), lambda i: (i, 0)),
        out_shape=jax.ShapeDtypeStruct((t, d), F32),
        compiler_params=pltpu.CompilerParams(
            dimension_semantics=("parallel",), vmem_limit_bytes=VMEM_LIMIT),
        name="ffn_ln",
    )(h2d, w13, w2, g.reshape(1, d), b.reshape(1, d))


_C_AQ, _C_AQR = 0, 512
_C_AK, _C_AKR = 1024, 1280
_C_AV = 1536
_C_IQ, _C_IQR = 1664, 2176
_C_IK, _C_IKR = 2688, 2816
_C_IW = 2944
_C_CQ = 3072
_C_CKV = 3328
_C_KR, _C_KRR = 3456, 3584
_C_GA, _C_GB = 3712, 4736
_C_END = 5760


def _proj_kernel(h_ref, w_ref, cosa_ref, sina_ref, cosm_ref, sinm_ref, qn_ref, kvn_ref,
                 wuq_ref, wukv_ref,
                 aq_ref, akd_ref, av_ref, iq_ref, ikd_ref, iw_ref, qm_ref, km_ref, vm_ref,
                 sa_ref, sb_ref, *, mla_scale):
    hb = h_ref[...].astype(BF16)

    def proj(c0, c1):
        return jnp.dot(hb, w_ref[:, c0:c1], preferred_element_type=F32)

    cosa = cosa_ref[...]
    sina = sina_ref[...]
    cosa4 = jnp.concatenate([cosa] * 4, axis=1)
    sina4 = jnp.concatenate([sina] * 4, axis=1)
    qscale = A_HEAD_DIM ** -0.5 * LOG2E
    aq = (proj(_C_AQ, _C_AQ + 512) * cosa4 + proj(_C_AQR, _C_AQR + 512) * sina4) * qscale
    aq_ref[0] = aq.astype(BF16)
    iscale = IDX_DIM ** -0.5
    iq = (proj(_C_IQ, _C_IQ + 512) * cosa4 + proj(_C_IQR, _C_IQR + 512) * sina4) * iscale
    iq_ref[0] = iq.astype(BF16)
    for g in range(A_KV_HEADS):
        ak = (proj(_C_AK + 128 * g, _C_AK + 128 * (g + 1)) * cosa
              + proj(_C_AKR + 128 * g, _C_AKR + 128 * (g + 1)) * sina)
        akd_ref[0, g] = ak.astype(BF16)
    av_ref[0] = proj(_C_AV, _C_AV + 128).astype(BF16)
    ik = proj(_C_IK, _C_IK + 128) * cosa + proj(_C_IKR, _C_IKR + 128) * sina
    ikd_ref[0] = ik.astype(BF16)
    iw_ref[0] = proj(_C_IW, _C_IW + 128) * (IDX_HEADS ** -0.5)
    sa_ref[...] = jax.nn.sigmoid(proj(_C_GA, _C_GA + 1024)).astype(BF16)
    sb_ref[...] = jax.nn.sigmoid(proj(_C_GB, _C_GB + 1024)).astype(BF16)

    cosm = cosm_ref[...]
    sinm = sinm_ref[...]
    cq = proj(_C_CQ, _C_CQ + B_Q_LORA)
    cq = cq * lax.rsqrt(jnp.mean(cq * cq, axis=-1, keepdims=True) + 1e-6) * qn_ref[...]
    cqb = cq.astype(BF16)
    for hd in range(B_HEADS):
        q = jnp.dot(cqb, wuq_ref[:, 128 * hd:128 * (hd + 1)], preferred_element_type=F32)
        qr = jnp.dot(cqb, wuq_ref[:, 1024 + 128 * hd:1024 + 128 * (hd + 1)],
                     preferred_element_type=F32)
        qm_ref[0, :, 128 * hd:128 * (hd + 1)] = ((q * cosm + qr * sinm) * mla_scale).astype(BF16)
    ckv = proj(_C_CKV, _C_CKV + B_KV_LORA)
    ckv = ckv * lax.rsqrt(jnp.mean(ckv * ckv, axis=-1, keepdims=True) + 1e-6) * kvn_ref[...]
    ckvb = ckv.astype(BF16)
    kr = proj(_C_KR, _C_KR + 128) * cosm + proj(_C_KRR, _C_KRR + 128) * sinm
    for hd in range(B_HEADS):
        kn = jnp.dot(ckvb, wukv_ref[:, 128 * hd:128 * (hd + 1)], preferred_element_type=F32)
        km_ref[0, hd] = (kn + kr).astype(BF16)
    vm_ref[0] = jnp.dot(ckvb, wukv_ref[:, 1024:1536], preferred_element_type=F32).astype(BF16)


def _rot_cols(w, heads, dim, rot):
    d_in = w.shape[0]
    w3 = w.reshape(d_in, heads, dim)
    half = rot // 2
    r = jnp.concatenate([-w3[:, :, half:rot], w3[:, :, :half],
                         jnp.zeros((d_in, heads, dim - rot), w.dtype)], axis=-1)
    return r.reshape(d_in, heads * dim)


def _prep_proj_weights(w_in, w_uq, w_ukv):
    d = w_in.shape[0]
    o = 0
    segs = {}
    for name, width in (("aq", 512), ("ak", 128), ("av", 128), ("iq", 512), ("ik", 64), ("iw", 8),
                        ("cq", 256), ("ckv", 128), ("kr", 32), ("ga", 1024), ("gb", 1024)):
        segs[name] = w_in[:, o:o + width]
        o += width
    z = lambda n: jnp.zeros((d, n), w_in.dtype)
    ak_r = _rot_cols(segs["ak"], A_KV_HEADS, A_HEAD_DIM, A_ROT)
    dup = lambda w, g: jnp.concatenate([w[:, 64 * g:64 * (g + 1)]] * 2, axis=1)
    ik_r = _rot_cols(segs["ik"], 1, IDX_DIM, IDX_ROT)
    kr_r = _rot_cols(segs["kr"], 1, B_ROPE, B_ROPE)
    cols = [
        segs["aq"], _rot_cols(segs["aq"], A_HEADS, A_HEAD_DIM, A_ROT),
        dup(segs["ak"], 0), dup(segs["ak"], 1), dup(ak_r, 0), dup(ak_r, 1),
        segs["av"],
        segs["iq"], _rot_cols(segs["iq"], IDX_HEADS, IDX_DIM, IDX_ROT),
        segs["ik"], segs["ik"], ik_r, ik_r,
        segs["iw"], z(120),
        segs["cq"], segs["ckv"],
        z(64), segs["kr"], z(32), z(64), kr_r, z(32),
        segs["ga"], segs["gb"],
    ]
    wcat = jnp.concatenate(cols, axis=1).astype(BF16)
    assert wcat.shape[1] == _C_END

    lq = w_uq.shape[0]
    uq3 = w_uq.reshape(lq, B_HEADS, B_NOPE + B_ROPE)
    uq_pad = jnp.concatenate([uq3, jnp.zeros((lq, B_HEADS, 32), w_uq.dtype)], axis=-1)
    rope_part = uq3[:, :, B_NOPE:]
    uq_rot = jnp.concatenate([jnp.zeros((lq, B_HEADS, B_NOPE), w_uq.dtype),
                              -rope_part[:, :, B_ROPE // 2:], rope_part[:, :, :B_ROPE // 2],
                              jnp.zeros((lq, B_HEADS, 32), w_uq.dtype)], axis=-1)
    wuq = jnp.concatenate([uq_pad.reshape(lq, 1024), uq_rot.reshape(lq, 1024)], axis=1).astype(BF16)

    lkv = w_ukv.shape[0]
    ukv3 = w_ukv.reshape(lkv, B_HEADS, B_NOPE + B_V)
    kn_pad = jnp.concatenate([ukv3[:, :, :B_NOPE], jnp.zeros((lkv, B_HEADS, 64), w_ukv.dtype)], axis=-1)
    wukv = jnp.concatenate([kn_pad.reshape(lkv, 1024), ukv3[:, :, B_NOPE:].reshape(lkv, 512)],
                           axis=1).astype(BF16)
    return wcat, wuq, wukv


def _rope_tables(lp):
    pos = jnp.arange(lp, dtype=F32)

    def cs(r):
        half = r // 2
        inv = ROPE_THETA ** (-(jnp.arange(half, dtype=F32) * 2.0 / r))
        ang = pos[:, None] * inv[None, :]
        return jnp.cos(ang), jnp.sin(ang)

    ca, sa = cs(A_ROT)
    one = lambda n: jnp.ones((lp, n), F32)
    zero = lambda n: jnp.zeros((lp, n), F32)
    cos64 = jnp.concatenate([ca, ca, one(A_HEAD_DIM - A_ROT)], axis=1)
    sin64 = jnp.concatenate([sa, sa, zero(A_HEAD_DIM - A_ROT)], axis=1)
    cosa = jnp.concatenate([cos64, cos64], axis=1)
    sina = jnp.concatenate([sin64, sin64], axis=1)
    cm, sm = cs(B_ROPE)
    cosm = jnp.concatenate([one(B_NOPE), cm, cm, one(32)], axis=1)
    sinm = jnp.concatenate([zero(B_NOPE), sm, sm, zero(32)], axis=1)
    return cosa, sina, cosm, sinm


def _projection(h2d, b, wcat, wuq, wukv, qn, kvn, tables):
    t, d = h2d.shape
    lp = t // b
    tm = _pick_tile(lp, (640, 512, 256, 128, 64, 32, 16))
    cosa, sina, cosm, sinm = tables
    const2 = lambda bi, i: (0, 0)
    row3 = lambda bi, i: (bi, i, 0)
    row2 = lambda bi, i: (bi * (lp // tm) + i, 0)
    tab = pl.BlockSpec((tm, 128), lambda bi, i: (i, 0))
    out_shapes = [
        jax.ShapeDtypeStruct((b, lp, 512), BF16),
        jax.ShapeDtypeStruct((b, A_KV_HEADS, lp, 128), BF16),
        jax.ShapeDtypeStruct((b, lp, 128), BF16),
        jax.ShapeDtypeStruct((b, lp, 512), BF16),
        jax.ShapeDtypeStruct((b, lp, 128), BF16),
        jax.ShapeDtypeStruct((b, lp, 128), F32),
        jax.ShapeDtypeStruct((b, lp, 1024), BF16),
        jax.ShapeDtypeStruct((b, B_HEADS, lp, 128), BF16),
        jax.ShapeDtypeStruct((b, lp, 512), BF16),
        jax.ShapeDtypeStruct((t, 1024), BF16),
        jax.ShapeDtypeStruct((t, 1024), BF16),
    ]
    out_specs = [
        pl.BlockSpec((1, tm, 512), row3),
        pl.BlockSpec((1, A_KV_HEADS, tm, 128), lambda bi, i: (bi, 0, i, 0)),
        pl.BlockSpec((1, tm, 128), row3),
        pl.BlockSpec((1, tm, 512), row3),
        pl.BlockSpec((1, tm, 128), row3),
        pl.BlockSpec((1, tm, 128), row3),
        pl.BlockSpec((1, tm, 1024), row3),
        pl.BlockSpec((1, B_HEADS, tm, 128), lambda bi, i: (bi, 0, i, 0)),
        pl.BlockSpec((1, tm, 512), row3),
        pl.BlockSpec((tm, 1024), row2),
        pl.BlockSpec((tm, 1024), row2),
    ]
    return pl.pallas_call(
        functools.partial(_proj_kernel, mla_scale=(B_NOPE + B_ROPE) ** -0.5 * LOG2E),
        grid=(b, lp // tm),
        in_specs=[
            pl.BlockSpec((tm, d), row2),
            pl.BlockSpec(wcat.shape, const2, pipeline_mode=pl.Buffered(1)),
            tab, tab, tab, tab,
            pl.BlockSpec((1, B_Q_LORA), const2),
            pl.BlockSpec((1, B_KV_LORA), const2),
            pl.BlockSpec(wuq.shape, const2, pipeline_mode=pl.Buffered(1)),
            pl.BlockSpec(wukv.shape, const2, pipeline_mode=pl.Buffered(1)),
        ],
        out_specs=out_specs,
        out_shape=out_shapes,
        compiler_params=pltpu.CompilerParams(
            dimension_semantics=("parallel", "parallel"), vmem_limit_bytes=VMEM_LIMIT),
        name="in_proj",
    )(h2d, wcat, cosa, sina, cosm, sinm, qn.reshape(1, -1), kvn.reshape(1, -1), wuq, wukv)


def _float_key(x):
    bits = lax.bitcast_convert_type(x, jnp.int32)
    return bits ^ ((bits >> 31) & 0x7FFFFFFF)


def _key_float(k):
    return lax.bitcast_convert_type(k ^ ((k >> 31) & 0x7FFFFFFF), F32)


def _finish_heads(acc_ref_g, n_heads, bq):
    outs = []
    for hh in range(n_heads):
        a = acc_ref_g[:, hh * bq:(hh + 1) * bq]
        o_t = a / a[B_V:B_V + 1, :]
        outs.append(jnp.transpose(o_t))
    return outs


def _pack_pairs(blocks):
    lane = lax.broadcasted_iota(jnp.int32, blocks[0].shape, 1)
    pairs = []
    for j in range(len(blocks) // 2):
        hi = pltpu.roll(blocks[2 * j + 1], 64, axis=1)
        pairs.append(jnp.where(lane < 64, blocks[2 * j], hi))
    return jnp.concatenate(pairs, axis=1) if len(pairs) > 1 else pairs[0]


def _attention_steps(n_steps, scores_fn, vt_fn, acc_ref, buf_refs, n_cols):
    ng = acc_ref.shape[0]

    def stage_a(j, buf):
        mx = []
        for g, x in enumerate(scores_fn(j)):
            buf[g] = x
            mx.append(jnp.max(x, axis=0, keepdims=True))
        return tuple(mx)

    def stage_b(j, buf, ms, mx):
        out = []
        for g in range(ng):
            m_new = jnp.maximum(ms[g], mx[g])
            p = jnp.exp2(buf[g] - m_new).astype(BF16)
            alpha = jnp.exp2(ms[g] - m_new)
            acc_ref[g] = acc_ref[g] * alpha + jnp.dot(vt_fn(j, g), p, preferred_element_type=F32)
            out.append(m_new)
        return tuple(out)

    def pair(t, carry):
        ms, mx = carry
        j = 2 * t
        mx1 = stage_a(j + 1, buf_refs[1])
        ms = stage_b(j, buf_refs[0], ms, mx)
        mx2 = stage_a(j + 2, buf_refs[0])
        ms = stage_b(j + 1, buf_refs[1], ms, mx1)
        return ms, mx2

    m0 = jnp.full((1, n_cols), M_INIT, F32)
    n_pairs = (n_steps - 1) // 2
    ms, mx = lax.fori_loop(0, n_pairs, pair, ((m0,) * ng, stage_a(0, buf_refs[0])))
    j = 2 * n_pairs
    one_more = j + 1 < n_steps

    @pl.when(one_more)
    def _():
        mx1 = stage_a(j + 1, buf_refs[1])
        ms1 = stage_b(j, buf_refs[0], ms, mx)
        stage_b(j + 1, buf_refs[1], ms1, mx1)

    @pl.when(jnp.logical_not(one_more))
    def _():
        stage_b(j, buf_refs[0], ms, mx)


def _dsa_kernel(iq_ref, iwt_ref, ikd_ref, aq_ref, akd_ref, avt_ref, o_ref,
                st_ref, qi_ref, qa_ref, acc_ref, c_ref, xa_ref, xb_ref, *, bq, w, topk):
    i = pl.program_id(1)
    n_steps = ((i + 1) * bq + (w - 1)) // w
    grp = A_HEADS // A_KV_HEADS

    def key_steps(body, init):
        return lax.fori_loop(0, n_steps, lambda j, c: body(pl.multiple_of(j * w, w), j, c), init)

    lane = lax.broadcasted_iota(jnp.int32, (bq, 128), 1)
    for hd in range(IDX_HEADS):
        keep = (lane < 64) if hd % 2 == 0 else (lane >= 64)
        pair = iq_ref[0, :, 128 * (hd // 2):128 * (hd // 2 + 1)].astype(F32)
        qi_ref[hd * bq:(hd + 1) * bq, :] = jnp.where(keep, pair, 0.0).astype(BF16)
        pair = aq_ref[0, :, 128 * (hd // 2):128 * (hd // 2 + 1)].astype(F32)
        qa_ref[hd // grp, (hd % grp) * bq:(hd % grp + 1) * bq, :] = jnp.where(keep, pair, 0.0).astype(BF16)

    qpos = i * bq + lax.broadcasted_iota(jnp.int32, (1, bq), 1)

    iwt = iwt_ref[0]

    def score_body(start, j, carry):
        kc = ikd_ref[0, pl.ds(start, w), :]
        x = lax.dot_general(kc, qi_ref[...], NT_DIMS, preferred_element_type=F32)
        acc = jnp.zeros((w, bq), F32)
        for hd in range(IDX_HEADS):
            acc = acc + jnp.maximum(x[:, hd * bq:(hd + 1) * bq], 0.0) * iwt[hd:hd + 1, :]
        kpos = start + lax.broadcasted_iota(jnp.int32, (w, bq), 0)
        st_ref[pl.ds(start, w), :] = jnp.where(kpos <= qpos, acc, NEG_INF)
        return carry

    key_steps(score_body, 0)

    def sweep(fn, init):
        def body(start, j, accs):
            s = st_ref[pl.ds(start, w), :]
            kpos0 = start + lax.broadcasted_iota(jnp.int32, (SUBLANES, bq), 0)
            accs = list(accs)
            for r in range(w // SUBLANES):
                lane_acc = fn(s[r * SUBLANES:(r + 1) * SUBLANES, :], kpos0 + r * SUBLANES, accs[r % 4])
                accs[r % 4] = lane_acc
            return tuple(accs)
        return key_steps(body, (init,) * 4)

    def count_where(pred):
        a = sweep(lambda s, kpos, acc: acc + jnp.where(pred(s, kpos), 1.0, 0.0), jnp.zeros((SUBLANES, bq), F32))
        return jnp.sum((a[0] + a[1]) + (a[2] + a[3]), axis=0, keepdims=True)

    def min_where(pred):
        a = sweep(lambda s, kpos, acc: jnp.minimum(acc, jnp.where(pred(s, kpos), s, jnp.inf)),
                  jnp.full((SUBLANES, bq), jnp.inf, F32))
        return jnp.min(jnp.minimum(jnp.minimum(a[0], a[1]), jnp.minimum(a[2], a[3])), axis=0, keepdims=True)

    def rows8(v):
        return jnp.broadcast_to(v, (SUBLANES, bq))

    a = sweep(lambda s, kpos, acc: jnp.maximum(acc, s), jnp.full((SUBLANES, bq), NEG_INF, F32))
    row_max = jnp.max(jnp.maximum(jnp.maximum(a[0], a[1]), jnp.maximum(a[2], a[3])), axis=0, keepdims=True)
    row_min = min_where(lambda s, kpos: s != NEG_INF)

    kf = float(topk)
    n_causal = (qpos + 1).astype(F32)
    need = n_causal > kf
    all_key = _float_key(jnp.full((1, bq), ALL_KEYS, F32))
    lo0 = jnp.where(need, _float_key(row_min), all_key)
    hi0 = jnp.where(need, _float_key(row_max), all_key)

    def is_active(lo, hi, cnt):
        return (lo < hi) & (cnt > kf + 1.0)

    def bracket_pass(state, use_key_mid):
        lo, hi, cnt = state
        act = is_active(lo, hi, cnt)
        if use_key_mid:
            mid = (lo >> 1) + (hi >> 1) + (((lo & 1) + (hi & 1) + 1) >> 1)
        else:
            fmid = 0.5 * _key_float(lo) + 0.5 * _key_float(hi)
            mid = jnp.minimum(jnp.maximum(_float_key(fmid), lo + 1), hi)
        cb = rows8(_key_float(mid))
        c_mid = count_where(lambda s, kpos: s >= cb)
        ge = c_mid >= kf
        up = act & ge
        dn = act & jnp.logical_not(ge)
        return (jnp.where(up, mid, lo), jnp.where(dn, mid - 1, hi), jnp.where(up, c_mid, cnt))

    def bracket_group(state):
        for p in range(4):
            state = bracket_pass(state, use_key_mid=(p == 3))
        return state

    def any_active(state):
        return jnp.max(is_active(*state).astype(jnp.int32)) > 0

    def key_above(m):
        km = _float_key(m)
        return jnp.where((km >= 0) & (km < 0x00800000), 0x00800000, km + 1)

    def tie_probe(state):
        lo, hi, cnt = state
        act = is_active(lo, hi, cnt)
        lo8 = rows8(_key_float(lo))
        m = min_where(lambda s, kpos: s >= lo8)
        m8 = rows8(m)
        n_gt = count_where(lambda s, kpos: s > m8)
        settled = act & (n_gt < kf)
        step = act & jnp.logical_not(settled)
        km = _float_key(m)
        return (jnp.where(settled, km, jnp.where(step, key_above(m), lo)), jnp.where(settled, km, hi),
                jnp.where(step, n_gt, cnt))

    first_rounds = 4
    state, _ = lax.while_loop(lambda c: any_active(c[0]) & (c[1] < first_rounds),
                              lambda c: (bracket_group(c[0]), c[1] + 1), ((lo0, hi0, n_causal), jnp.int32(0)))
    state = lax.cond(any_active(state), tie_probe, lambda st: st, state)
    lo, hi, cnt = lax.while_loop(any_active, bracket_group, state)
    c_lo = _key_float(lo)
    c_lo8 = rows8(c_lo)
    m_row = min_where(lambda s, kpos: s >= c_lo8)
    c_sel = jnp.where(need & (cnt > kf), _key_float(key_above(m_row)), c_lo)
    c_sel8 = rows8(c_sel)
    n_sel = count_where(lambda s, kpos: s >= c_sel8)
    tie = need & (n_sel != kf)
    c_ref[...] = c_sel

    @pl.when(jnp.max(tie.astype(jnp.int32)) > 0)
    def _():
        want = kf - n_sel
        lower = jnp.where(lax.broadcasted_iota(jnp.int32, (w, w), 0) >= lax.broadcasted_iota(jnp.int32, (w, w), 1),
                          1.0, 0.0).astype(BF16)

        def drop_body(start, j, seen):
            s = st_ref[pl.ds(start, w), :]
            is_m = tie & (s == m_row)
            rank = seen + jnp.dot(lower, jnp.where(is_m, 1.0, 0.0).astype(BF16), preferred_element_type=F32)
            st_ref[pl.ds(start, w), :] = jnp.where(is_m & (rank > want), NEG_INF, s)
            return rank[w - 1:w, :]

        key_steps(drop_body, jnp.zeros((1, bq), F32))
        c_ref[...] = jnp.where(tie, m_row, c_sel)

    acc_ref[...] = jnp.zeros(acc_ref.shape, F32)
    c_fin = c_ref[...]

    def att_scores(j):
        start = pl.multiple_of(j * w, w)
        bias = jnp.where(st_ref[pl.ds(start, w), :] >= c_fin, 0.0, NEG_INF)
        bias4 = jnp.concatenate([bias] * grp, axis=1)
        return [lax.dot_general(akd_ref[0, g, pl.ds(start, w), :], qa_ref[g], NT_DIMS,
                                preferred_element_type=F32) + bias4 for g in range(A_KV_HEADS)]

    _attention_steps(n_steps, att_scores, lambda j, g: avt_ref[0, g, j], acc_ref, (xa_ref, xb_ref), grp * bq)

    blocks = []
    for g in range(A_KV_HEADS):
        blocks += _finish_heads(acc_ref.at[g], grp, bq)
    o_ref[...] = _pack_pairs(blocks).astype(BF16)


def _vt_blocks(v, heads, blk):
    b, lp, _ = v.shape
    v4 = v.reshape(b, lp, heads, 64)
    ext = jnp.concatenate([v4, jnp.ones((b, lp, heads, 1), v.dtype),
                           jnp.zeros((b, lp, heads, 63), v.dtype)], axis=-1)
    ext = ext.reshape(b, lp // blk, blk, heads, 128)
    return jnp.transpose(ext, (0, 3, 1, 4, 2))


def _dsa_attention(iq, iwt, ikd, aq, akd, av, topk):
    b, lp, _ = iq.shape
    bq = BLOCK_Q
    nq = lp // bq
    w = _pick_tile(lp, (640, 512, 384, 256, 128))
    avt = _vt_blocks(av, A_KV_HEADS, w)
    grp = A_HEADS // A_KV_HEADS
    return pl.pallas_call(
        functools.partial(_dsa_kernel, bq=bq, w=w, topk=topk),
        grid=(b, nq),
        in_specs=[
            pl.BlockSpec((1, bq, 512), lambda bi, i: (bi, i, 0)),
            pl.BlockSpec((1, IDX_HEADS, bq), lambda bi, i: (bi, 0, i)),
            pl.BlockSpec((1, lp, 128), lambda bi, i: (bi, 0, 0)),
            pl.BlockSpec((1, bq, 512), lambda bi, i: (bi, i, 0)),
            pl.BlockSpec((1, A_KV_HEADS, lp, 128), lambda bi, i: (bi, 0, 0, 0)),
            pl.BlockSpec((1, A_KV_HEADS, lp // w, 128, w), lambda bi, i: (bi, 0, 0, 0, 0)),
        ],
        out_specs=pl.BlockSpec((bq, 512), lambda bi, i: (bi * nq + i, 0)),
        out_shape=jax.ShapeDtypeStruct((b * lp, 512), BF16),
        scratch_shapes=[
            pltpu.VMEM((lp, bq), F32),
            pltpu.VMEM((IDX_HEADS * bq, 128), BF16),
            pltpu.VMEM((A_KV_HEADS, grp * bq, 128), BF16),
            pltpu.VMEM((A_KV_HEADS, 128, grp * bq), F32),
            pltpu.VMEM((1, bq), F32),
            pltpu.VMEM((A_KV_HEADS, w, grp * bq), F32),
            pltpu.VMEM((A_KV_HEADS, w, grp * bq), F32),
        ],
        compiler_params=pltpu.CompilerParams(
            dimension_semantics=("parallel", "arbitrary"), vmem_limit_bytes=VMEM_LIMIT),
        name="dsa_attention",
    )(iq, iwt, ikd, aq, akd, avt)


def _mla_kernel(q_ref, k_ref, vt_ref, o_ref, acc_ref, tri_ref, xa_ref, xb_ref, *, bq):
    i = pl.program_id(2)
    acc_ref[...] = jnp.zeros(acc_ref.shape, F32)
    kpos = lax.broadcasted_iota(jnp.int32, (bq, bq), 0)
    qpos = lax.broadcasted_iota(jnp.int32, (bq, bq), 1)
    tri_ref[0] = jnp.zeros((bq, bq), F32)
    tri_ref[1] = jnp.where(kpos <= qpos, 0.0, NEG_INF)

    def scores(j):
        start = pl.multiple_of(j * bq, bq)
        bias = tri_ref[(j == i).astype(jnp.int32)]
        return [lax.dot_general(k_ref[0, hh, pl.ds(start, bq), :], q_ref[0, :, 128 * hh:128 * (hh + 1)],
                                NT_DIMS, preferred_element_type=F32) + bias for hh in range(2)]

    _attention_steps(i + 1, scores, lambda j, hh: vt_ref[0, hh, j], acc_ref, (xa_ref, xb_ref), bq)
    blocks = _finish_heads(acc_ref.at[0], 1, bq) + _finish_heads(acc_ref.at[1], 1, bq)
    o_ref[...] = _pack_pairs(blocks).astype(BF16)


def _mla_attention(qm, km, vm):
    b, lp, _ = qm.shape
    bq = _pick_tile(lp, (640, 512, 256, 128))
    nq = lp // bq
    vmt = _vt_blocks(vm, B_HEADS, bq)
    return pl.pallas_call(
        functools.partial(_mla_kernel, bq=bq),
        grid=(b, B_HEADS // 2, nq),
        in_specs=[
            pl.BlockSpec((1, bq, 256), lambda bi, p, i: (bi, i, p)),
            pl.BlockSpec((1, 2, lp, 128), lambda bi, p, i: (bi, p, 0, 0)),
            pl.BlockSpec((1, 2, nq, 128, bq), lambda bi, p, i: (bi, p, 0, 0, 0)),
        ],
        out_specs=pl.BlockSpec((bq, 128), lambda bi, p, i: (bi * nq + i, p)),
        out_shape=jax.ShapeDtypeStruct((b * lp, 512), BF16),
        scratch_shapes=[pltpu.VMEM((2, 128, bq), F32), pltpu.VMEM((2, bq, bq), F32),
                        pltpu.VMEM((2, bq, bq), F32), pltpu.VMEM((2, bq, bq), F32)],
        compiler_params=pltpu.CompilerParams(
            dimension_semantics=("parallel", "parallel", "arbitrary"), vmem_limit_bytes=VMEM_LIMIT),
        name="mla_attention",
    )(qm, km, vmt)


def _merge_kernel(h_ref, oa_ref, ob_ref, sa_ref, sb_ref, wa_ref, wb_ref, wo_ref, g_ref, b_ref, o_ref,
                  *, alpha):
    pa = jnp.dot(oa_ref[...], wa_ref[...], preferred_element_type=F32)
    pb = jnp.dot(ob_ref[...], wb_ref[...], preferred_element_type=F32)
    mixed = sa_ref[...].astype(F32) * pa + sb_ref[...].astype(F32) * pb
    y = alpha * h_ref[...] + jnp.dot(mixed.astype(BF16), wo_ref[...], preferred_element_type=F32)
    o_ref[...] = _layer_norm_rows(y, g_ref[...], b_ref[...])


def _merge_ln(h2d, oa, ob, sa, sb, wa, wb, wo, g, b, alpha):
    t, d = h2d.shape
    tm = _pick_tile(t, (512, 256, 128, 64, 32, 16, 8))
    row = lambda i: (i, 0)
    const = lambda i: (0, 0)
    return pl.pallas_call(
        functools.partial(_merge_kernel, alpha=alpha),
        grid=(t // tm,),
        in_specs=[
            pl.BlockSpec((tm, d), row),
            pl.BlockSpec((tm, oa.shape[1]), row),
            pl.BlockSpec((tm, ob.shape[1]), row),
            pl.BlockSpec((tm, d), row),
            pl.BlockSpec((tm, d), row),
            pl.BlockSpec(wa.shape, const),
            pl.BlockSpec(wb.shape, const),
            pl.BlockSpec(wo.shape, const),
            pl.BlockSpec((1, d), const),
            pl.BlockSpec((1, d), const),
        ],
        out_specs=pl.BlockSpec((tm, d), row),
        out_shape=jax.ShapeDtypeStruct((t, d), F32),
        compiler_params=pltpu.CompilerParams(
            dimension_semantics=("parallel",), vmem_limit_bytes=VMEM_LIMIT),
        name="merge_ln",
    )(h2d, oa, ob, sa, sb, wa, wb, wo, g.reshape(1, d), b.reshape(1, d))


def kernel(x, meta_tokens, ln_g, ln_b, ffn1_w13, ffn1_w2, w_in, mla_q_norm, mla_kv_norm, mla_w_uq,
           mla_w_ukv, w_branch_a, w_branch_b, w_out, ffn2_w13, ffn2_w2):
    b, s, d = x.shape
    depth = ln_g.shape[0]
    alpha = (2 * depth) ** 0.25
    total = s + N_META
    lp = -(-total // BLOCK_Q) * BLOCK_Q
    topk = min(TOPK_MAX, s // 4)
    meta = jnp.broadcast_to(meta_tokens[None].astype(x.dtype), (b, N_META, d))
    h = jnp.concatenate([meta, x, jnp.zeros((b, lp - total, d), x.dtype)], axis=1).reshape(b * lp, d)
    tables = _rope_tables(lp)
    t = b * lp
    for l in range(depth):
        h2 = _ffn_ln(h, ffn1_w13[l].astype(BF16), ffn1_w2[l].astype(BF16),
                     ln_g[l, 0], ln_b[l, 0], alpha)
        wcat, wuq, wukv = _prep_proj_weights(w_in[l], mla_w_uq[l], mla_w_ukv[l])
        (aq, akd, av, iq, ikd, iw, qm, km, vm, sa, sb) = _projection(
            h2, b, wcat, wuq, wukv, mla_q_norm[l], mla_kv_norm[l], tables)
        iwt = jnp.swapaxes(iw[:, :, :IDX_HEADS], 1, 2)
        o_a = _dsa_attention(iq, iwt, ikd, aq, akd, av, topk)
        o_b = _mla_attention(qm, km, vm)
        h3 = _merge_ln(h2, o_a, o_b, sa, sb,
                       w_branch_a[l].astype(BF16), w_branch_b[l].astype(BF16), w_out[l].astype(BF16),
                       ln_g[l, 1], ln_b[l, 1], alpha)
        h = _ffn_ln(h3, ffn2_w13[l].astype(BF16), ffn2_w2[l].astype(BF16), ln_g[l, 2], ln_b[l, 2], alpha)
    return h.reshape(b, lp, d)[:, N_META:N_META + s]
```

```python
import functools

import jax
import jax.numpy as jnp
from jax import lax
from jax.experimental import pallas as pl
from jax.experimental.pallas import tpu as pltpu

N_META = 16
ROPE_THETA = 500000.0
BLOCK_Q = 128
TOPK_MAX = 256
A_HEADS = 8
A_KV_HEADS = 2
A_HEAD_DIM = 64
A_ROT = A_HEAD_DIM // 4
IDX_HEADS = 8
IDX_DIM = 64
IDX_ROT = IDX_DIM // 4
B_HEADS = 8
B_NOPE = 64
B_ROPE = 32
B_V = 64
B_Q_LORA = 256
B_KV_LORA = 128

LANES = 128
SUBLANES = 8
VMEM_LIMIT = 56 * 1024 * 1024

F32 = jnp.float32
BF16 = jnp.bfloat16
NEG_INF = float("-inf")
M_INIT = -1e30
ALL_KEYS = -3e38
LOG2E = 1.4426950408889634
NT_DIMS = (((1,), (1,)), ((), ()))


def _pick_tile(n, candidates):
    for c in candidates:
        if n % c == 0:
            return c
    return n


def _key_step(lp):
    return _pick_tile(lp, (640, 512, 256, 128))


def _layer_norm_rows(y, g, b):
    mu = jnp.mean(y, axis=-1, keepdims=True)
    d = y - mu
    var = jnp.mean(d * d, axis=-1, keepdims=True)
    return d * lax.rsqrt(var + 1e-5) * g + b


def _ffn_kernel(x_ref, w13_ref, w2_ref, g_ref, b_ref, o_ref, *, alpha, d_ff, chunk):
    x = x_ref[...]
    xb = x.astype(BF16)
    acc = jnp.zeros(x.shape, F32)
    for c in range(d_ff // chunk):
        gate = jnp.dot(xb, w13_ref[:, c * chunk:(c + 1) * chunk], preferred_element_type=F32)
        up = jnp.dot(xb, w13_ref[:, d_ff + c * chunk:d_ff + (c + 1) * chunk],
                     preferred_element_type=F32)
        act = (gate * jax.nn.sigmoid(gate) * up).astype(BF16)
        acc = acc + jnp.dot(act, w2_ref[c * chunk:(c + 1) * chunk, :], preferred_element_type=F32)
    y = alpha * x + 0.5 * acc
    o_ref[...] = _layer_norm_rows(y, g_ref[...], b_ref[...])


def _ffn_ln(h2d, w13, w2, g, b, alpha):
    t, d = h2d.shape
    d_ff = w2.shape[0]
    tm = _pick_tile(t, (512, 256, 128, 64, 32, 16, 8))
    chunk = _pick_tile(d_ff, (1408, 1024, 512, 256, 128))
    const = lambda i: (0, 0)
    return pl.pallas_call(
        functools.partial(_ffn_kernel, alpha=alpha, d_ff=d_ff, chunk=chunk),
        grid=(t // tm,),
        in_specs=[
            pl.BlockSpec((tm, d), lambda i: (i, 0)),
            pl.BlockSpec((d, 2 * d_ff), const, pipeline_mode=pl.Buffered(1)),
            pl.BlockSpec((d_ff, d), const, pipeline_mode=pl.Buffered(1)),
            pl.BlockSpec((1, d), const),
            pl.BlockSpec((1, d), const),
        ],
        out_specs=pl.BlockSpec((tm, d), lambda i: (i, 0)),
        out_shape=jax.ShapeDtypeStruct((t, d), F32),
        compiler_params=pltpu.CompilerParams(
            dimension_semantics=("parallel",), vmem_limit_bytes=VMEM_LIMIT),
        name="ffn_ln",
    )(h2d, w13, w2, g.reshape(1, d), b.reshape(1, d))


_C_AQ, _C_AQR = 0, 512
_C_AK, _C_AKR = 1024, 1280
_C_AV = 1536
_C_IQ, _C_IQR = 1664, 2176
_C_IK, _C_IKR = 2688, 2816
_C_IW = 2944
_C_CQ = 3072
_C_CKV = 3328
_C_KR, _C_KRR = 3456, 3584
_C_GA, _C_GB = 3712, 4736
_C_END = 5760


def _store_vt(v, vt_ref, heads):
    lane = lax.broadcasted_iota(jnp.int32, (v.shape[0], 128), 1)
    tail = jnp.where(lane == B_V, 1.0, 0.0)
    for hd in range(heads):
        pair = v[:, 128 * (hd // 2):128 * (hd // 2 + 1)]
        if hd % 2 == 1:
            pair = pltpu.roll(pair, 64, axis=1)
        vt_ref[0, hd, 0] = jnp.transpose(jnp.where(lane < B_V, pair, tail)).astype(BF16)


def _proj_kernel(h_ref, w_ref, cosa_ref, sina_ref, cosm_ref, sinm_ref, qn_ref, kvn_ref,
                 wuq_ref, wukv_ref,
                 aq_ref, akd_ref, avt_ref, iq_ref, ikd_ref, iw_ref, qm_ref, km_ref, vmt_ref,
                 sa_ref, sb_ref, *, mla_scale):
    hb = h_ref[...].astype(BF16)

    def proj(c0, c1):
        return jnp.dot(hb, w_ref[:, c0:c1], preferred_element_type=F32)

    cosa = cosa_ref[...]
    sina = sina_ref[...]
    cosa4 = jnp.concatenate([cosa] * 4, axis=1)
    sina4 = jnp.concatenate([sina] * 4, axis=1)
    qscale = A_HEAD_DIM ** -0.5 * LOG2E
    aq = (proj(_C_AQ, _C_AQ + 512) * cosa4 + proj(_C_AQR, _C_AQR + 512) * sina4) * qscale
    aq_ref[0] = aq.astype(BF16)
    iscale = IDX_DIM ** -0.5
    iq = (proj(_C_IQ, _C_IQ + 512) * cosa4 + proj(_C_IQR, _C_IQR + 512) * sina4) * iscale
    iq_ref[0] = iq.astype(BF16)
    for g in range(A_KV_HEADS):
        ak = (proj(_C_AK + 128 * g, _C_AK + 128 * (g + 1)) * cosa
              + proj(_C_AKR + 128 * g, _C_AKR + 128 * (g + 1)) * sina)
        akd_ref[0, g] = ak.astype(BF16)
    _store_vt(proj(_C_AV, _C_AV + 128), avt_ref, A_KV_HEADS)
    ik = proj(_C_IK, _C_IK + 128) * cosa + proj(_C_IKR, _C_IKR + 128) * sina
    ikd_ref[0] = ik.astype(BF16)
    iw_ref[0] = proj(_C_IW, _C_IW + 128) * (IDX_HEADS ** -0.5)
    sa_ref[...] = jax.nn.sigmoid(proj(_C_GA, _C_GA + 1024)).astype(BF16)
    sb_ref[...] = jax.nn.sigmoid(proj(_C_GB, _C_GB + 1024)).astype(BF16)

    cosm = cosm_ref[...]
    sinm = sinm_ref[...]
    cq = proj(_C_CQ, _C_CQ + B_Q_LORA)
    cq = cq * lax.rsqrt(jnp.mean(cq * cq, axis=-1, keepdims=True) + 1e-6) * qn_ref[...]
    cqb = cq.astype(BF16)
    for hd in range(B_HEADS):
        q = jnp.dot(cqb, wuq_ref[:, 128 * hd:128 * (hd + 1)], preferred_element_type=F32)
        qr = jnp.dot(cqb, wuq_ref[:, 1024 + 128 * hd:1024 + 128 * (hd + 1)],
                     preferred_element_type=F32)
        qm_ref[0, :, 128 * hd:128 * (hd + 1)] = ((q * cosm + qr * sinm) * mla_scale).astype(BF16)
    ckv = proj(_C_CKV, _C_CKV + B_KV_LORA)
    ckv = ckv * lax.rsqrt(jnp.mean(ckv * ckv, axis=-1, keepdims=True) + 1e-6) * kvn_ref[...]
    ckvb = ckv.astype(BF16)
    kr = proj(_C_KR, _C_KR + 128) * cosm + proj(_C_KRR, _C_KRR + 128) * sinm
    for hd in range(B_HEADS):
        kn = jnp.dot(ckvb, wukv_ref[:, 128 * hd:128 * (hd + 1)], preferred_element_type=F32)
        km_ref[0, hd] = (kn + kr).astype(BF16)
    _store_vt(jnp.dot(ckvb, wukv_ref[:, 1024:1536], preferred_element_type=F32), vmt_ref, B_HEADS)


def _rot_cols(w, heads, dim, rot):
    d_in = w.shape[0]
    w3 = w.reshape(d_in, heads, dim)
    half = rot // 2
    r = jnp.concatenate([-w3[:, :, half:rot], w3[:, :, :half],
                         jnp.zeros((d_in, heads, dim - rot), w.dtype)], axis=-1)
    return r.reshape(d_in, heads * dim)


def _prep_proj_weights(w_in, w_uq, w_ukv):
    d = w_in.shape[0]
    o = 0
    segs = {}
    for name, width in (("aq", 512), ("ak", 128), ("av", 128), ("iq", 512), ("ik", 64), ("iw", 8),
                        ("cq", 256), ("ckv", 128), ("kr", 32), ("ga", 1024), ("gb", 1024)):
        segs[name] = w_in[:, o:o + width]
        o += width
    z = lambda n: jnp.zeros((d, n), w_in.dtype)
    ak_r = _rot_cols(segs["ak"], A_KV_HEADS, A_HEAD_DIM, A_ROT)
    dup = lambda w, g: jnp.concatenate([w[:, 64 * g:64 * (g + 1)]] * 2, axis=1)
    ik_r = _rot_cols(segs["ik"], 1, IDX_DIM, IDX_ROT)
    kr_r = _rot_cols(segs["kr"], 1, B_ROPE, B_ROPE)
    cols = [
        segs["aq"], _rot_cols(segs["aq"], A_HEADS, A_HEAD_DIM, A_ROT),
        dup(segs["ak"], 0), dup(segs["ak"], 1), dup(ak_r, 0), dup(ak_r, 1),
        segs["av"],
        segs["iq"], _rot_cols(segs["iq"], IDX_HEADS, IDX_DIM, IDX_ROT),
        segs["ik"], segs["ik"], ik_r, ik_r,
        segs["iw"], z(120),
        segs["cq"], segs["ckv"],
        z(64), segs["kr"], z(32), z(64), kr_r, z(32),
        segs["ga"], segs["gb"],
    ]
    wcat = jnp.concatenate(cols, axis=1).astype(BF16)
    assert wcat.shape[1] == _C_END

    lq = w_uq.shape[0]
    uq3 = w_uq.reshape(lq, B_HEADS, B_NOPE + B_ROPE)
    uq_pad = jnp.concatenate([uq3, jnp.zeros((lq, B_HEADS, 32), w_uq.dtype)], axis=-1)
    rope_part = uq3[:, :, B_NOPE:]
    uq_rot = jnp.concatenate([jnp.zeros((lq, B_HEADS, B_NOPE), w_uq.dtype),
                              -rope_part[:, :, B_ROPE // 2:], rope_part[:, :, :B_ROPE // 2],
                              jnp.zeros((lq, B_HEADS, 32), w_uq.dtype)], axis=-1)
    wuq = jnp.concatenate([uq_pad.reshape(lq, 1024), uq_rot.reshape(lq, 1024)], axis=1).astype(BF16)

    lkv = w_ukv.shape[0]
    ukv3 = w_ukv.reshape(lkv, B_HEADS, B_NOPE + B_V)
    kn_pad = jnp.concatenate([ukv3[:, :, :B_NOPE], jnp.zeros((lkv, B_HEADS, 64), w_ukv.dtype)], axis=-1)
    wukv = jnp.concatenate([kn_pad.reshape(lkv, 1024), ukv3[:, :, B_NOPE:].reshape(lkv, 512)],
                           axis=1).astype(BF16)
    return wcat, wuq, wukv


def _rope_tables(lp):
    pos = jnp.arange(lp, dtype=F32)

    def cs(r):
        half = r // 2
        inv = ROPE_THETA ** (-(jnp.arange(half, dtype=F32) * 2.0 / r))
        ang = pos[:, None] * inv[None, :]
        return jnp.cos(ang), jnp.sin(ang)

    ca, sa = cs(A_ROT)
    one = lambda n: jnp.ones((lp, n), F32)
    zero = lambda n: jnp.zeros((lp, n), F32)
    cos64 = jnp.concatenate([ca, ca, one(A_HEAD_DIM - A_ROT)], axis=1)
    sin64 = jnp.concatenate([sa, sa, zero(A_HEAD_DIM - A_ROT)], axis=1)
    cosa = jnp.concatenate([cos64, cos64], axis=1)
    sina = jnp.concatenate([sin64, sin64], axis=1)
    cm, sm = cs(B_ROPE)
    cosm = jnp.concatenate([one(B_NOPE), cm, cm, one(32)], axis=1)
    sinm = jnp.concatenate([zero(B_NOPE), sm, sm, zero(32)], axis=1)
    return cosa, sina, cosm, sinm


def _projection(h2d, b, wcat, wuq, wukv, qn, kvn, tables):
    t, d = h2d.shape
    lp = t // b
    tm = _key_step(lp)
    cosa, sina, cosm, sinm = tables
    const2 = lambda bi, i: (0, 0)
    row3 = lambda bi, i: (bi, i, 0)
    row2 = lambda bi, i: (bi * (lp // tm) + i, 0)
    tab = pl.BlockSpec((tm, 128), lambda bi, i: (i, 0))
    out_shapes = [
        jax.ShapeDtypeStruct((b, lp, 512), BF16),
        jax.ShapeDtypeStruct((b, A_KV_HEADS, lp, 128), BF16),
        jax.ShapeDtypeStruct((b, A_KV_HEADS, lp // tm, 128, tm), BF16),
        jax.ShapeDtypeStruct((b, lp, 512), BF16),
        jax.ShapeDtypeStruct((b, lp, 128), BF16),
        jax.ShapeDtypeStruct((b, lp, 128), F32),
        jax.ShapeDtypeStruct((b, lp, 1024), BF16),
        jax.ShapeDtypeStruct((b, B_HEADS, lp, 128), BF16),
        jax.ShapeDtypeStruct((b, B_HEADS, lp // tm, 128, tm), BF16),
        jax.ShapeDtypeStruct((t, 1024), BF16),
        jax.ShapeDtypeStruct((t, 1024), BF16),
    ]
    out_specs = [
        pl.BlockSpec((1, tm, 512), row3),
        pl.BlockSpec((1, A_KV_HEADS, tm, 128), lambda bi, i: (bi, 0, i, 0)),
        pl.BlockSpec((1, A_KV_HEADS, 1, 128, tm), lambda bi, i: (bi, 0, i, 0, 0)),
        pl.BlockSpec((1, tm, 512), row3),
        pl.BlockSpec((1, tm, 128), row3),
        pl.BlockSpec((1, tm, 128), row3),
        pl.BlockSpec((1, tm, 1024), row3),
        pl.BlockSpec((1, B_HEADS, tm, 128), lambda bi, i: (bi, 0, i, 0)),
        pl.BlockSpec((1, B_HEADS, 1, 128, tm), lambda bi, i: (bi, 0, i, 0, 0)),
        pl.BlockSpec((tm, 1024), row2),
        pl.BlockSpec((tm, 1024), row2),
    ]
    return pl.pallas_call(
        functools.partial(_proj_kernel, mla_scale=(B_NOPE + B_ROPE) ** -0.5 * LOG2E),
        grid=(b, lp // tm),
        in_specs=[
            pl.BlockSpec((tm, d), row2),
            pl.BlockSpec(wcat.shape, const2, pipeline_mode=pl.Buffered(1)),
            tab, tab, tab, tab,
            pl.BlockSpec((1, B_Q_LORA), const2),
            pl.BlockSpec((1, B_KV_LORA), const2),
            pl.BlockSpec(wuq.shape, const2, pipeline_mode=pl.Buffered(1)),
            pl.BlockSpec(wukv.shape, const2, pipeline_mode=pl.Buffered(1)),
        ],
        out_specs=out_specs,
        out_shape=out_shapes,
        compiler_params=pltpu.CompilerParams(
            dimension_semantics=("parallel", "parallel"), vmem_limit_bytes=VMEM_LIMIT),
        name="in_proj",
    )(h2d, wcat, cosa, sina, cosm, sinm, qn.reshape(1, -1), kvn.reshape(1, -1), wuq, wukv)


def _float_key(x):
    bits = lax.bitcast_convert_type(x, jnp.int32)
    return bits ^ ((bits >> 31) & 0x7FFFFFFF)


def _key_float(k):
    return lax.bitcast_convert_type(k ^ ((k >> 31) & 0x7FFFFFFF), F32)


def _finish_heads(acc_ref_g, n_heads, bq):
    outs = []
    for hh in range(n_heads):
        a = acc_ref_g[:, hh * bq:(hh + 1) * bq]
        o_t = a / a[B_V:B_V + 1, :]
        outs.append(jnp.transpose(o_t))
    return outs


def _pack_pairs(blocks):
    lane = lax.broadcasted_iota(jnp.int32, blocks[0].shape, 1)
    pairs = []
    for j in range(len(blocks) // 2):
        hi = pltpu.roll(blocks[2 * j + 1], 64, axis=1)
        pairs.append(jnp.where(lane < 64, blocks[2 * j], hi))
    return jnp.concatenate(pairs, axis=1) if len(pairs) > 1 else pairs[0]


def _pipelined_steps(n_steps, stage_a, stage_b, buf_refs, carry):
    def pair(t, c):
        carry, aux = c
        j = 2 * t
        aux1 = stage_a(j + 1, buf_refs[1])
        carry = stage_b(j, buf_refs[0], carry, aux)
        aux2 = stage_a(j + 2, buf_refs[0])
        carry = stage_b(j + 1, buf_refs[1], carry, aux1)
        return carry, aux2

    n_pairs = (n_steps - 1) // 2
    carry, aux = lax.fori_loop(0, n_pairs, pair, (carry, stage_a(0, buf_refs[0])))
    j = 2 * n_pairs

    def two_left(carry, aux):
        aux1 = stage_a(j + 1, buf_refs[1])
        carry = stage_b(j, buf_refs[0], carry, aux)
        return stage_b(j + 1, buf_refs[1], carry, aux1)

    def one_left(carry, aux):
        return stage_b(j, buf_refs[0], carry, aux)

    return lax.cond(j + 1 < n_steps, two_left, one_left, carry, aux)


def _attention_steps(n_steps, scores_fn, vt_fn, acc_ref, buf_refs, n_cols):
    ng = acc_ref.shape[0]

    def stage_a(j, buf):
        mx = []
        for g, x in enumerate(scores_fn(j)):
            buf[g] = x
            mx.append(jnp.max(x, axis=0, keepdims=True))
        return tuple(mx)

    def stage_b(j, buf, ms, mx):
        out = []
        for g in range(ng):
            m_new = jnp.maximum(ms[g], mx[g])
            p = jnp.exp2(buf[g] - m_new).astype(BF16)
            alpha = jnp.exp2(ms[g] - m_new)
            acc_ref[g] = acc_ref[g] * alpha + jnp.dot(vt_fn(j, g), p, preferred_element_type=F32)
            out.append(m_new)
        return tuple(out)

    m0 = jnp.full((1, n_cols), M_INIT, F32)
    _pipelined_steps(n_steps, stage_a, stage_b, buf_refs, (m0,) * ng)


def _dsa_kernel(iq_ref, iwt_ref, ikd_ref, aq_ref, akd_ref, avt_ref, o_ref,
                st_ref, qi_ref, qa_ref, acc_ref, c_ref, xa_ref, xb_ref, *, bq, w, topk):
    i = pl.program_id(1)
    n_steps = ((i + 1) * bq + (w - 1)) // w
    grp = A_HEADS // A_KV_HEADS

    def key_steps(body, init):
        return lax.fori_loop(0, n_steps, lambda j, c: body(pl.multiple_of(j * w, w), j, c), init)

    lane = lax.broadcasted_iota(jnp.int32, (bq, 128), 1)
    for hd in range(IDX_HEADS):
        keep = (lane < 64) if hd % 2 == 0 else (lane >= 64)
        pair = iq_ref[0, :, 128 * (hd // 2):128 * (hd // 2 + 1)].astype(F32)
        qi_ref[hd * bq:(hd + 1) * bq, :] = jnp.where(keep, pair, 0.0).astype(BF16)
        pair = aq_ref[0, :, 128 * (hd // 2):128 * (hd // 2 + 1)].astype(F32)
        qa_ref[hd // grp, (hd % grp) * bq:(hd % grp + 1) * bq, :] = jnp.where(keep, pair, 0.0).astype(BF16)

    qpos = i * bq + lax.broadcasted_iota(jnp.int32, (1, bq), 1)

    iwt = iwt_ref[0]

    def dots_stage(j, buf):
        kc = ikd_ref[0, pl.ds(pl.multiple_of(j * w, w), w), :]
        for half in range(2):
            buf[half] = lax.dot_general(kc, qi_ref[half * grp * bq:(half + 1) * grp * bq, :], NT_DIMS,
                                        preferred_element_type=F32)
        return ()

    def combine_stage(j, buf, carry, aux):
        start = pl.multiple_of(j * w, w)
        acc = jnp.zeros((w, bq), F32)
        for hd in range(IDX_HEADS):
            x = buf[hd // grp, :, (hd % grp) * bq:(hd % grp + 1) * bq]
            acc = acc + jnp.maximum(x, 0.0) * iwt[hd:hd + 1, :]
        kpos = start + lax.broadcasted_iota(jnp.int32, (w, bq), 0)
        causal = kpos <= qpos
        st_ref[pl.ds(start, w), :] = jnp.where(causal, acc, NEG_INF)
        mn, mx = carry
        lo_s = jnp.where(causal, acc, jnp.inf)
        hi_s = jnp.where(causal, acc, NEG_INF)
        for r in range(w // SUBLANES):
            mn = jnp.minimum(mn, lo_s[r * SUBLANES:(r + 1) * SUBLANES, :])
            mx = jnp.maximum(mx, hi_s[r * SUBLANES:(r + 1) * SUBLANES, :])
        return mn, mx

    mn8, mx8 = _pipelined_steps(n_steps, dots_stage, combine_stage, (xa_ref, xb_ref),
                                (jnp.full((SUBLANES, bq), jnp.inf, F32), jnp.full((SUBLANES, bq), NEG_INF, F32)))
    row_min = jnp.min(mn8, axis=0, keepdims=True)
    row_max = jnp.max(mx8, axis=0, keepdims=True)

    def sweep(fn, init):
        def body(start, j, accs):
            s = st_ref[pl.ds(start, w), :]
            kpos0 = start + lax.broadcasted_iota(jnp.int32, (SUBLANES, bq), 0)
            accs = list(accs)
            for r in range(w // SUBLANES):
                lane_acc = fn(s[r * SUBLANES:(r + 1) * SUBLANES, :], kpos0 + r * SUBLANES, accs[r % 4])
                accs[r % 4] = lane_acc
            return tuple(accs)
        return key_steps(body, (init,) * 4)

    def count_where(pred):
        a = sweep(lambda s, kpos, acc: acc + jnp.where(pred(s, kpos), 1.0, 0.0), jnp.zeros((SUBLANES, bq), F32))
        return jnp.sum((a[0] + a[1]) + (a[2] + a[3]), axis=0, keepdims=True)

    def min_where(pred):
        a = sweep(lambda s, kpos, acc: jnp.minimum(acc, jnp.where(pred(s, kpos), s, jnp.inf)),
                  jnp.full((SUBLANES, bq), jnp.inf, F32))
        return jnp.min(jnp.minimum(jnp.minimum(a[0], a[1]), jnp.minimum(a[2], a[3])), axis=0, keepdims=True)

    def rows8(v):
        return jnp.broadcast_to(v, (SUBLANES, bq))

    kf = float(topk)
    n_causal = (qpos + 1).astype(F32)
    need = n_causal > kf
    all_key = _float_key(jnp.full((1, bq), ALL_KEYS, F32))
    lo0 = jnp.where(need, _float_key(row_min), all_key)
    hi0 = jnp.where(need, _float_key(row_max), all_key)

    def is_active(lo, hi, cnt):
        return (lo < hi) & (cnt > kf + 1.0)

    def bracket_pass(state, use_key_mid):
        lo, hi, cnt = state
        act = is_active(lo, hi, cnt)
        if use_key_mid:
            mid = (lo >> 1) + (hi >> 1) + (((lo & 1) + (hi & 1) + 1) >> 1)
        else:
            fmid = 0.5 * _key_float(lo) + 0.5 * _key_float(hi)
            mid = jnp.minimum(jnp.maximum(_float_key(fmid), lo + 1), hi)
        cb = rows8(_key_float(mid))
        c_mid = count_where(lambda s, kpos: s >= cb)
        ge = c_mid >= kf
        up = act & ge
        dn = act & jnp.logical_not(ge)
        return (jnp.where(up, mid, lo), jnp.where(dn, mid - 1, hi), jnp.where(up, c_mid, cnt))

    def bracket_group(state):
        for p in range(4):
            state = bracket_pass(state, use_key_mid=(p == 3))
        return state

    def any_active(state):
        return jnp.max(is_active(*state).astype(jnp.int32)) > 0

    def key_above(m):
        km = _float_key(m)
        return jnp.where((km >= 0) & (km < 0x00800000), 0x00800000, km + 1)

    def tie_probe(state):
        lo, hi, cnt = state
        act = is_active(lo, hi, cnt)
        lo8 = rows8(_key_float(lo))
        m = min_where(lambda s, kpos: s >= lo8)
        m8 = rows8(m)
        n_gt = count_where(lambda s, kpos: s > m8)
        settled = act & (n_gt < kf)
        step = act & jnp.logical_not(settled)
        km = _float_key(m)
        return (jnp.where(settled, km, jnp.where(step, key_above(m), lo)), jnp.where(settled, km, hi),
                jnp.where(step, n_gt, cnt))

    first_rounds = 4
    state, _ = lax.while_loop(lambda c: any_active(c[0]) & (c[1] < first_rounds),
                              lambda c: (bracket_group(c[0]), c[1] + 1), ((lo0, hi0, n_causal), jnp.int32(0)))
    state = lax.cond(any_active(state), tie_probe, lambda st: st, state)
    lo, hi, cnt = lax.while_loop(any_active, bracket_group, state)
    c_lo = _key_float(lo)
    c_lo8 = rows8(c_lo)
    m_row = min_where(lambda s, kpos: s >= c_lo8)
    c_sel = jnp.where(need & (cnt > kf), _key_float(key_above(m_row)), c_lo)
    c_sel8 = rows8(c_sel)
    n_sel = count_where(lambda s, kpos: s >= c_sel8)
    tie = need & (n_sel != kf)
    c_ref[...] = c_sel

    @pl.when(jnp.max(tie.astype(jnp.int32)) > 0)
    def _():
        want = kf - n_sel
        lower = jnp.where(lax.broadcasted_iota(jnp.int32, (w, w), 0) >= lax.broadcasted_iota(jnp.int32, (w, w), 1),
                          1.0, 0.0).astype(BF16)

        def drop_body(start, j, seen):
            s = st_ref[pl.ds(start, w), :]
            is_m = tie & (s == m_row)
            rank = seen + jnp.dot(lower, jnp.where(is_m, 1.0, 0.0).astype(BF16), preferred_element_type=F32)
            st_ref[pl.ds(start, w), :] = jnp.where(is_m & (rank > want), NEG_INF, s)
            return rank[w - 1:w, :]

        key_steps(drop_body, jnp.zeros((1, bq), F32))
        c_ref[...] = jnp.where(tie, m_row, c_sel)

    acc_ref[...] = jnp.zeros(acc_ref.shape, F32)
    c_fin = c_ref[...]

    def att_scores(j):
        start = pl.multiple_of(j * w, w)
        bias = jnp.where(st_ref[pl.ds(start, w), :] >= c_fin, 0.0, NEG_INF)
        bias4 = jnp.concatenate([bias] * grp, axis=1)
        return [lax.dot_general(akd_ref[0, g, pl.ds(start, w), :], qa_ref[g], NT_DIMS,
                                preferred_element_type=F32) + bias4 for g in range(A_KV_HEADS)]

    _attention_steps(n_steps, att_scores, lambda j, g: avt_ref[0, g, j], acc_ref, (xa_ref, xb_ref), grp * bq)

    blocks = []
    for g in range(A_KV_HEADS):
        blocks += _finish_heads(acc_ref.at[g], grp, bq)
    o_ref[...] = _pack_pairs(blocks).astype(BF16)


def _dsa_attention(iq, iwt, ikd, aq, akd, avt, topk):
    b, lp, _ = iq.shape
    bq = BLOCK_Q
    nq = lp // bq
    w = _key_step(lp)
    grp = A_HEADS // A_KV_HEADS
    return pl.pallas_call(
        functools.partial(_dsa_kernel, bq=bq, w=w, topk=topk),
        grid=(b, nq),
        in_specs=[
            pl.BlockSpec((1, bq, 512), lambda bi, i: (bi, i, 0)),
            pl.BlockSpec((1, IDX_HEADS, bq), lambda bi, i: (bi, 0, i)),
            pl.BlockSpec((1, lp, 128), lambda bi, i: (bi, 0, 0)),
            pl.BlockSpec((1, bq, 512), lambda bi, i: (bi, i, 0)),
            pl.BlockSpec((1, A_KV_HEADS, lp, 128), lambda bi, i: (bi, 0, 0, 0)),
            pl.BlockSpec((1, A_KV_HEADS, lp // w, 128, w), lambda bi, i: (bi, 0, 0, 0, 0)),
        ],
        out_specs=pl.BlockSpec((bq, 512), lambda bi, i: (bi * nq + i, 0)),
        out_shape=jax.ShapeDtypeStruct((b * lp, 512), BF16),
        scratch_shapes=[
            pltpu.VMEM((lp, bq), F32),
            pltpu.VMEM((IDX_HEADS * bq, 128), BF16),
            pltpu.VMEM((A_KV_HEADS, grp * bq, 128), BF16),
            pltpu.VMEM((A_KV_HEADS, 128, grp * bq), F32),
            pltpu.VMEM((1, bq), F32),
            pltpu.VMEM((A_KV_HEADS, w, grp * bq), F32),
            pltpu.VMEM((A_KV_HEADS, w, grp * bq), F32),
        ],
        compiler_params=pltpu.CompilerParams(
            dimension_semantics=("parallel", "arbitrary"), vmem_limit_bytes=VMEM_LIMIT),
        name="dsa_attention",
    )(iq, iwt, ikd, aq, akd, avt)


def _mla_kernel(q_ref, k_ref, vt_ref, o_ref, acc_ref, tri_ref, xa_ref, xb_ref, *, bq):
    i = pl.program_id(2)
    acc_ref[...] = jnp.zeros(acc_ref.shape, F32)
    kpos = lax.broadcasted_iota(jnp.int32, (bq, bq), 0)
    qpos = lax.broadcasted_iota(jnp.int32, (bq, bq), 1)
    tri_ref[0] = jnp.zeros((bq, bq), F32)
    tri_ref[1] = jnp.where(kpos <= qpos, 0.0, NEG_INF)

    def scores(j):
        start = pl.multiple_of(j * bq, bq)
        bias = tri_ref[(j == i).astype(jnp.int32)]
        return [lax.dot_general(k_ref[0, hh, pl.ds(start, bq), :], q_ref[0, :, 128 * hh:128 * (hh + 1)],
                                NT_DIMS, preferred_element_type=F32) + bias for hh in range(2)]

    _attention_steps(i + 1, scores, lambda j, hh: vt_ref[0, hh, j], acc_ref, (xa_ref, xb_ref), bq)
    blocks = _finish_heads(acc_ref.at[0], 1, bq) + _finish_heads(acc_ref.at[1], 1, bq)
    o_ref[...] = _pack_pairs(blocks).astype(BF16)


def _mla_attention(qm, km, vmt):
    b, lp, _ = qm.shape
    bq = _key_step(lp)
    nq = lp // bq
    return pl.pallas_call(
        functools.partial(_mla_kernel, bq=bq),
        grid=(b, B_HEADS // 2, nq),
        in_specs=[
            pl.BlockSpec((1, bq, 256), lambda bi, p, i: (bi, i, p)),
            pl.BlockSpec((1, 2, lp, 128), lambda bi, p, i: (bi, p, 0, 0)),
            pl.BlockSpec((1, 2, nq, 128, bq), lambda bi, p, i: (bi, p, 0, 0, 0)),
        ],
        out_specs=pl.BlockSpec((bq, 128), lambda bi, p, i: (bi * nq + i, p)),
        out_shape=jax.ShapeDtypeStruct((b * lp, 512), BF16),
        scratch_shapes=[pltpu.VMEM((2, 128, bq), F32), pltpu.VMEM((2, bq, bq), F32),
                        pltpu.VMEM((2, bq, bq), F32), pltpu.VMEM((2, bq, bq), F32)],
        compiler_params=pltpu.CompilerParams(
            dimension_semantics=("parallel", "parallel", "arbitrary"), vmem_limit_bytes=VMEM_LIMIT),
        name="mla_attention",
    )(qm, km, vmt)


def _merge_kernel(h_ref, oa_ref, ob_ref, sa_ref, sb_ref, wa_ref, wb_ref, wo_ref, g_ref, b_ref, o_ref,
                  *, alpha):
    pa = jnp.dot(oa_ref[...], wa_ref[...], preferred_element_type=F32)
    pb = jnp.dot(ob_ref[...], wb_ref[...], preferred_element_type=F32)
    mixed = sa_ref[...].astype(F32) * pa + sb_ref[...].astype(F32) * pb
    y = alpha * h_ref[...] + jnp.dot(mixed.astype(BF16), wo_ref[...], preferred_element_type=F32)
    o_ref[...] = _layer_norm_rows(y, g_ref[...], b_ref[...])


def _merge_ln(h2d, oa, ob, sa, sb, wa, wb, wo, g, b, alpha):
    t, d = h2d.shape
    tm = _pick_tile(t, (512, 256, 128, 64, 32, 16, 8))
    row = lambda i: (i, 0)
    const = lambda i: (0, 0)
    return pl.pallas_call(
        functools.partial(_merge_kernel, alpha=alpha),
        grid=(t // tm,),
        in_specs=[
            pl.BlockSpec((tm, d), row),
            pl.BlockSpec((tm, oa.shape[1]), row),
            pl.BlockSpec((tm, ob.shape[1]), row),
            pl.BlockSpec((tm, d), row),
            pl.BlockSpec((tm, d), row),
            pl.BlockSpec(wa.shape, const),
            pl.BlockSpec(wb.shape, const),
            pl.BlockSpec(wo.shape, const),
            pl.BlockSpec((1, d), const),
            pl.BlockSpec((1, d), const),
        ],
        out_specs=pl.BlockSpec((tm, d), row),
        out_shape=jax.ShapeDtypeStruct((t, d), F32),
        compiler_params=pltpu.CompilerParams(
            dimension_semantics=("parallel",), vmem_limit_bytes=VMEM_LIMIT),
        name="merge_ln",
    )(h2d, oa, ob, sa, sb, wa, wb, wo, g.reshape(1, d), b.reshape(1, d))


def kernel(x, meta_tokens, ln_g, ln_b, ffn1_w13, ffn1_w2, w_in, mla_q_norm, mla_kv_norm, mla_w_uq,
           mla_w_ukv, w_branch_a, w_branch_b, w_out, ffn2_w13, ffn2_w2):
    b, s, d = x.shape
    depth = ln_g.shape[0]
    alpha = (2 * depth) ** 0.25
    total = s + N_META
    lp = -(-total // BLOCK_Q) * BLOCK_Q
    topk = min(TOPK_MAX, s // 4)
    meta = jnp.broadcast_to(meta_tokens[None].astype(x.dtype), (b, N_META, d))
    h = jnp.concatenate([meta, x, jnp.zeros((b, lp - total, d), x.dtype)], axis=1).reshape(b * lp, d)
    tables = _rope_tables(lp)
    t = b * lp
    for l in range(depth):
        h2 = _ffn_ln(h, ffn1_w13[l].astype(BF16), ffn1_w2[l].astype(BF16),
                     ln_g[l, 0], ln_b[l, 0], alpha)
        wcat, wuq, wukv = _prep_proj_weights(w_in[l], mla_w_uq[l], mla_w_ukv[l])
        (aq, akd, avt, iq, ikd, iw, qm, km, vmt, sa, sb) = _projection(
            h2, b, wcat, wuq, wukv, mla_q_norm[l], mla_kv_norm[l], tables)
        iwt = jnp.swapaxes(iw[:, :, :IDX_HEADS], 1, 2)
        o_a = _dsa_attention(iq, iwt, ikd, aq, akd, avt, topk)
        o_b = _mla_attention(qm, km, vmt)
        h3 = _merge_ln(h2, o_a, o_b, sa, sb,
                       w_branch_a[l].astype(BF16), w_branch_b[l].astype(BF16), w_out[l].astype(BF16),
                       ln_g[l, 1], ln_b[l, 1], alpha)
        h = _ffn_ln(h3, ffn2_w13[l].astype(BF16), ffn2_w2[l].astype(BF16), ln_g[l, 2], ln_b[l, 2], alpha)
    return h.reshape(b, lp, d)[:, N_META:N_META + s]
```

```python
import functools

import jax
import jax.numpy as jnp
from jax import lax
from jax.experimental import pallas as pl
from jax.experimental.pallas import tpu as pltpu

N_META = 16
ROPE_THETA = 500000.0
BLOCK_Q = 128
TOPK_MAX = 256
A_HEADS = 8
A_KV_HEADS = 2
A_HEAD_DIM = 64
A_ROT = A_HEAD_DIM // 4
IDX_HEADS = 8
IDX_DIM = 64
IDX_ROT = IDX_DIM // 4
B_HEADS = 8
B_NOPE = 64
B_ROPE = 32
B_V = 64
B_Q_LORA = 256
B_KV_LORA = 128

LANES = 128
SUBLANES = 8
VMEM_LIMIT = 56 * 1024 * 1024

F32 = jnp.float32
BF16 = jnp.bfloat16
NEG_INF = float("-inf")
M_INIT = -1e30
ALL_KEYS = -3e38
LOG2E = 1.4426950408889634
NT_DIMS = (((1,), (1,)), ((), ()))


def _pick_tile(n, candidates):
    for c in candidates:
        if n % c == 0:
            return c
    return n


def _key_step(lp):
    return _pick_tile(lp, (640, 512, 256, 128))


def _layer_norm_rows(y, g, b):
    mu = jnp.mean(y, axis=-1, keepdims=True)
    d = y - mu
    var = jnp.mean(d * d, axis=-1, keepdims=True)
    return d * lax.rsqrt(var + 1e-5) * g + b


def _ffn_kernel(x_ref, w13_ref, w2_ref, g_ref, b_ref, o_ref, *, alpha, d_ff, chunk):
    x = x_ref[...]
    xb = x.astype(BF16)
    acc = jnp.zeros(x.shape, F32)
    for c in range(d_ff // chunk):
        gate = jnp.dot(xb, w13_ref[:, c * chunk:(c + 1) * chunk], preferred_element_type=F32)
        up = jnp.dot(xb, w13_ref[:, d_ff + c * chunk:d_ff + (c + 1) * chunk],
                     preferred_element_type=F32)
        act = (gate * jax.nn.sigmoid(gate) * up).astype(BF16)
        acc = acc + jnp.dot(act, w2_ref[c * chunk:(c + 1) * chunk, :], preferred_element_type=F32)
    y = alpha * x + 0.5 * acc
    o_ref[...] = _layer_norm_rows(y, g_ref[...], b_ref[...])


def _ffn_ln(h2d, w13, w2, g, b, alpha):
    t, d = h2d.shape
    d_ff = w2.shape[0]
    tm = _pick_tile(t, (512, 256, 128, 64, 32, 16, 8))
    chunk = _pick_tile(d_ff, (1408, 1024, 512, 256, 128))
    const = lambda i: (0, 0)
    return pl.pallas_call(
        functools.partial(_ffn_kernel, alpha=alpha, d_ff=d_ff, chunk=chunk),
        grid=(t // tm,),
        in_specs=[
            pl.BlockSpec((tm, d), lambda i: (i, 0)),
            pl.BlockSpec((d, 2 * d_ff), const, pipeline_mode=pl.Buffered(1)),
            pl.BlockSpec((d_ff, d), const, pipeline_mode=pl.Buffered(1)),
            pl.BlockSpec((1, d), const),
            pl.BlockSpec((1, d), const),
        ],
        out_specs=pl.BlockSpec((tm, d), lambda i: (i, 0)),
        out_shape=jax.ShapeDtypeStruct((t, d), F32),
        compiler_params=pltpu.CompilerParams(
            dimension_semantics=("parallel",), vmem_limit_bytes=VMEM_LIMIT),
        name="ffn_ln",
    )(h2d, w13, w2, g.reshape(1, d), b.reshape(1, d))


_C_AQ = 0
_C_AK = 512
_C_AV = 768
_C_IQ = 896
_C_IK = 1408
_C_IW = 1536
_C_CQ = 1664
_C_CKV = 1920
_C_KR, _C_KRR = 2048, 2176
_C_GA, _C_GB = 2304, 3328
_C_END = 4352


def _store_vt(v, vt_ref, heads):
    lane = lax.broadcasted_iota(jnp.int32, (v.shape[0], 128), 1)
    tail = jnp.where(lane == B_V, 1.0, 0.0)
    for hd in range(heads):
        pair = v[:, 128 * (hd // 2):128 * (hd // 2 + 1)]
        if hd % 2 == 1:
            pair = pltpu.roll(pair, 64, axis=1)
        vt_ref[0, hd, 0] = jnp.transpose(jnp.where(lane < B_V, pair, tail)).astype(BF16)


def _proj_kernel(h_ref, w_ref, cosa_ref, sina_ref, cosm_ref, sinm_ref, qn_ref, kvn_ref,
                 wuq_ref, wukv_ref,
                 aq_ref, akd_ref, avt_ref, iq_ref, ikd_ref, iw_ref, qm_ref, km_ref, vmt_ref,
                 sa_ref, sb_ref, *, mla_scale):
    hb = h_ref[...].astype(BF16)

    def proj(c0, c1):
        return jnp.dot(hb, w_ref[:, c0:c1], preferred_element_type=F32)

    cosa = cosa_ref[...]
    sina = sina_ref[...]

    def rope16(x):
        n = x.shape[1]
        first = (lax.broadcasted_iota(jnp.int32, x.shape, 1) & (A_HEAD_DIM - 1)) < A_ROT // 2
        partner = jnp.where(first, pltpu.roll(x, n - A_ROT // 2, axis=1), pltpu.roll(x, A_ROT // 2, axis=1))
        reps = n // 128
        cos = cosa if reps == 1 else jnp.concatenate([cosa] * reps, axis=1)
        sin = sina if reps == 1 else jnp.concatenate([sina] * reps, axis=1)
        return x * cos + partner * sin

    aq_ref[0] = (rope16(proj(_C_AQ, _C_AQ + 512)) * (A_HEAD_DIM ** -0.5 * LOG2E)).astype(BF16)
    iq_ref[0] = (rope16(proj(_C_IQ, _C_IQ + 512)) * (IDX_DIM ** -0.5)).astype(BF16)
    for g in range(A_KV_HEADS):
        akd_ref[0, g] = rope16(proj(_C_AK + 128 * g, _C_AK + 128 * (g + 1))).astype(BF16)
    _store_vt(proj(_C_AV, _C_AV + 128), avt_ref, A_KV_HEADS)
    ikd_ref[0] = rope16(proj(_C_IK, _C_IK + 128)).astype(BF16)
    iw_ref[0] = proj(_C_IW, _C_IW + 128) * (IDX_HEADS ** -0.5)
    sa_ref[...] = jax.nn.sigmoid(proj(_C_GA, _C_GA + 1024)).astype(BF16)
    sb_ref[...] = jax.nn.sigmoid(proj(_C_GB, _C_GB + 1024)).astype(BF16)

    cosm = cosm_ref[...]
    sinm = sinm_ref[...]
    cq = proj(_C_CQ, _C_CQ + B_Q_LORA)
    cq = cq * lax.rsqrt(jnp.mean(cq * cq, axis=-1, keepdims=True) + 1e-6) * qn_ref[...]
    cqb = cq.astype(BF16)
    for hd in range(B_HEADS):
        q = jnp.dot(cqb, wuq_ref[:, 128 * hd:128 * (hd + 1)], preferred_element_type=F32)
        qr = jnp.dot(cqb, wuq_ref[:, 1024 + 128 * hd:1024 + 128 * (hd + 1)],
                     preferred_element_type=F32)
        qm_ref[0, :, 128 * hd:128 * (hd + 1)] = ((q * cosm + qr * sinm) * mla_scale).astype(BF16)
    ckv = proj(_C_CKV, _C_CKV + B_KV_LORA)
    ckv = ckv * lax.rsqrt(jnp.mean(ckv * ckv, axis=-1, keepdims=True) + 1e-6) * kvn_ref[...]
    ckvb = ckv.astype(BF16)
    kr = proj(_C_KR, _C_KR + 128) * cosm + proj(_C_KRR, _C_KRR + 128) * sinm
    for hd in range(B_HEADS):
        kn = jnp.dot(ckvb, wukv_ref[:, 128 * hd:128 * (hd + 1)], preferred_element_type=F32)
        km_ref[0, hd] = (kn + kr).astype(BF16)
    _store_vt(jnp.dot(ckvb, wukv_ref[:, 1024:1536], preferred_element_type=F32), vmt_ref, B_HEADS)


def _rot_cols(w, heads, dim, rot):
    d_in = w.shape[0]
    w3 = w.reshape(d_in, heads, dim)
    half = rot // 2
    r = jnp.concatenate([-w3[:, :, half:rot], w3[:, :, :half],
                         jnp.zeros((d_in, heads, dim - rot), w.dtype)], axis=-1)
    return r.reshape(d_in, heads * dim)


def _prep_proj_weights(w_in, w_uq, w_ukv):
    d = w_in.shape[0]
    o = 0
    segs = {}
    for name, width in (("aq", 512), ("ak", 128), ("av", 128), ("iq", 512), ("ik", 64), ("iw", 8),
                        ("cq", 256), ("ckv", 128), ("kr", 32), ("ga", 1024), ("gb", 1024)):
        segs[name] = w_in[:, o:o + width]
        o += width
    z = lambda n: jnp.zeros((d, n), w_in.dtype)
    dup = lambda w, g: jnp.concatenate([w[:, 64 * g:64 * (g + 1)]] * 2, axis=1)
    kr_r = _rot_cols(segs["kr"], 1, B_ROPE, B_ROPE)
    cols = [
        segs["aq"],
        dup(segs["ak"], 0), dup(segs["ak"], 1),
        segs["av"],
        segs["iq"],
        segs["ik"], segs["ik"],
        segs["iw"], z(120),
        segs["cq"], segs["ckv"],
        z(64), segs["kr"], z(32), z(64), kr_r, z(32),
        segs["ga"], segs["gb"],
    ]
    wcat = jnp.concatenate(cols, axis=1).astype(BF16)
    assert wcat.shape[1] == _C_END

    lq = w_uq.shape[0]
    uq3 = w_uq.reshape(lq, B_HEADS, B_NOPE + B_ROPE)
    uq_pad = jnp.concatenate([uq3, jnp.zeros((lq, B_HEADS, 32), w_uq.dtype)], axis=-1)
    rope_part = uq3[:, :, B_NOPE:]
    uq_rot = jnp.concatenate([jnp.zeros((lq, B_HEADS, B_NOPE), w_uq.dtype),
                              -rope_part[:, :, B_ROPE // 2:], rope_part[:, :, :B_ROPE // 2],
                              jnp.zeros((lq, B_HEADS, 32), w_uq.dtype)], axis=-1)
    wuq = jnp.concatenate([uq_pad.reshape(lq, 1024), uq_rot.reshape(lq, 1024)], axis=1).astype(BF16)

    lkv = w_ukv.shape[0]
    ukv3 = w_ukv.reshape(lkv, B_HEADS, B_NOPE + B_V)
    kn_pad = jnp.concatenate([ukv3[:, :, :B_NOPE], jnp.zeros((lkv, B_HEADS, 64), w_ukv.dtype)], axis=-1)
    wukv = jnp.concatenate([kn_pad.reshape(lkv, 1024), ukv3[:, :, B_NOPE:].reshape(lkv, 512)],
                           axis=1).astype(BF16)
    return wcat, wuq, wukv


def _rope_tables(lp):
    pos = jnp.arange(lp, dtype=F32)

    def cs(r):
        half = r // 2
        inv = ROPE_THETA ** (-(jnp.arange(half, dtype=F32) * 2.0 / r))
        ang = pos[:, None] * inv[None, :]
        return jnp.cos(ang), jnp.sin(ang)

    ca, sa = cs(A_ROT)
    one = lambda n: jnp.ones((lp, n), F32)
    zero = lambda n: jnp.zeros((lp, n), F32)
    cos64 = jnp.concatenate([ca, ca, one(A_HEAD_DIM - A_ROT)], axis=1)
    sin64 = jnp.concatenate([-sa, sa, zero(A_HEAD_DIM - A_ROT)], axis=1)
    cosa = jnp.concatenate([cos64, cos64], axis=1)
    sina = jnp.concatenate([sin64, sin64], axis=1)
    cm, sm = cs(B_ROPE)
    cosm = jnp.concatenate([one(B_NOPE), cm, cm, one(32)], axis=1)
    sinm = jnp.concatenate([zero(B_NOPE), sm, sm, zero(32)], axis=1)
    return cosa, sina, cosm, sinm


def _projection(h2d, b, wcat, wuq, wukv, qn, kvn, tables):
    t, d = h2d.shape
    lp = t // b
    tm = _key_step(lp)
    cosa, sina, cosm, sinm = tables
    const2 = lambda bi, i: (0, 0)
    row3 = lambda bi, i: (bi, i, 0)
    row2 = lambda bi, i: (bi * (lp // tm) + i, 0)
    tab = pl.BlockSpec((tm, 128), lambda bi, i: (i, 0))
    out_shapes = [
        jax.ShapeDtypeStruct((b, lp, 512), BF16),
        jax.ShapeDtypeStruct((b, A_KV_HEADS, lp, 128), BF16),
        jax.ShapeDtypeStruct((b, A_KV_HEADS, lp // tm, 128, tm), BF16),
        jax.ShapeDtypeStruct((b, lp, 512), BF16),
        jax.ShapeDtypeStruct((b, lp, 128), BF16),
        jax.ShapeDtypeStruct((b, lp, 128), F32),
        jax.ShapeDtypeStruct((b, lp, 1024), BF16),
        jax.ShapeDtypeStruct((b, B_HEADS, lp, 128), BF16),
        jax.ShapeDtypeStruct((b, B_HEADS, lp // tm, 128, tm), BF16),
        jax.ShapeDtypeStruct((t, 1024), BF16),
        jax.ShapeDtypeStruct((t, 1024), BF16),
    ]
    out_specs = [
        pl.BlockSpec((1, tm, 512), row3),
        pl.BlockSpec((1, A_KV_HEADS, tm, 128), lambda bi, i: (bi, 0, i, 0)),
        pl.BlockSpec((1, A_KV_HEADS, 1, 128, tm), lambda bi, i: (bi, 0, i, 0, 0)),
        pl.BlockSpec((1, tm, 512), row3),
        pl.BlockSpec((1, tm, 128), row3),
        pl.BlockSpec((1, tm, 128), row3),
        pl.BlockSpec((1, tm, 1024), row3),
        pl.BlockSpec((1, B_HEADS, tm, 128), lambda bi, i: (bi, 0, i, 0)),
        pl.BlockSpec((1, B_HEADS, 1, 128, tm), lambda bi, i: (bi, 0, i, 0, 0)),
        pl.BlockSpec((tm, 1024), row2),
        pl.BlockSpec((tm, 1024), row2),
    ]
    return pl.pallas_call(
        functools.partial(_proj_kernel, mla_scale=(B_NOPE + B_ROPE) ** -0.5 * LOG2E),
        grid=(b, lp // tm),
        in_specs=[
            pl.BlockSpec((tm, d), row2),
            pl.BlockSpec(wcat.shape, const2, pipeline_mode=pl.Buffered(1)),
            tab, tab, tab, tab,
            pl.BlockSpec((1, B_Q_LORA), const2),
            pl.BlockSpec((1, B_KV_LORA), const2),
            pl.BlockSpec(wuq.shape, const2, pipeline_mode=pl.Buffered(1)),
            pl.BlockSpec(wukv.shape, const2, pipeline_mode=pl.Buffered(1)),
        ],
        out_specs=out_specs,
        out_shape=out_shapes,
        compiler_params=pltpu.CompilerParams(
            dimension_semantics=("parallel", "parallel"), vmem_limit_bytes=VMEM_LIMIT),
        name="in_proj",
    )(h2d, wcat, cosa, sina, cosm, sinm, qn.reshape(1, -1), kvn.reshape(1, -1), wuq, wukv)


def _float_key(x):
    bits = lax.bitcast_convert_type(x, jnp.int32)
    return bits ^ ((bits >> 31) & 0x7FFFFFFF)


def _key_float(k):
    return lax.bitcast_convert_type(k ^ ((k >> 31) & 0x7FFFFFFF), F32)


def _finish_heads(acc_ref_g, n_heads, bq):
    outs = []
    for hh in range(n_heads):
        a = acc_ref_g[:, hh * bq:(hh + 1) * bq]
        o_t = a / a[B_V:B_V + 1, :]
        outs.append(jnp.transpose(o_t))
    return outs


def _pack_pairs(blocks):
    lane = lax.broadcasted_iota(jnp.int32, blocks[0].shape, 1)
    pairs = []
    for j in range(len(blocks) // 2):
        hi = pltpu.roll(blocks[2 * j + 1], 64, axis=1)
        pairs.append(jnp.where(lane < 64, blocks[2 * j], hi))
    return jnp.concatenate(pairs, axis=1) if len(pairs) > 1 else pairs[0]


def _pipelined_steps(n_steps, stage_a, stage_b, buf_refs, carry):
    def pair(t, c):
        carry, aux = c
        j = 2 * t
        aux1 = stage_a(j + 1, buf_refs[1])
        carry = stage_b(j, buf_refs[0], carry, aux)
        aux2 = stage_a(j + 2, buf_refs[0])
        carry = stage_b(j + 1, buf_refs[1], carry, aux1)
        return carry, aux2

    n_pairs = (n_steps - 1) // 2
    carry, aux = lax.fori_loop(0, n_pairs, pair, (carry, stage_a(0, buf_refs[0])))
    j = 2 * n_pairs

    def two_left(carry, aux):
        aux1 = stage_a(j + 1, buf_refs[1])
        carry = stage_b(j, buf_refs[0], carry, aux)
        return stage_b(j + 1, buf_refs[1], carry, aux1)

    def one_left(carry, aux):
        return stage_b(j, buf_refs[0], carry, aux)

    return lax.cond(j + 1 < n_steps, two_left, one_left, carry, aux)


def _attention_steps(n_steps, scores_fn, vt_fn, acc_ref, buf_refs, n_cols):
    ng = acc_ref.shape[0]

    def stage_a(j, buf):
        mx = []
        for g, x in enumerate(scores_fn(j)):
            buf[g] = x
            mx.append(jnp.max(x, axis=0, keepdims=True))
        return tuple(mx)

    def stage_b(j, buf, ms, mx):
        out = []
        for g in range(ng):
            m_new = jnp.maximum(ms[g], mx[g])
            p = jnp.exp2(buf[g] - m_new).astype(BF16)
            alpha = jnp.exp2(ms[g] - m_new)
            acc_ref[g] = acc_ref[g] * alpha + jnp.dot(vt_fn(j, g), p, preferred_element_type=F32)
            out.append(m_new)
        return tuple(out)

    m0 = jnp.full((1, n_cols), M_INIT, F32)
    _pipelined_steps(n_steps, stage_a, stage_b, buf_refs, (m0,) * ng)


def _dsa_kernel(iq_ref, iwt_ref, ikd_ref, aq_ref, akd_ref, avt_ref, o_ref,
                st_ref, qi_ref, qa_ref, acc_ref, c_ref, xa_ref, xb_ref, *, bq, w, topk):
    i = pl.program_id(1)
    n_steps = ((i + 1) * bq + (w - 1)) // w
    grp = A_HEADS // A_KV_HEADS

    def key_steps(body, init):
        return lax.fori_loop(0, n_steps, lambda j, c: body(pl.multiple_of(j * w, w), j, c), init)

    lane = lax.broadcasted_iota(jnp.int32, (bq, 128), 1)
    for hd in range(IDX_HEADS):
        keep = (lane < 64) if hd % 2 == 0 else (lane >= 64)
        pair = iq_ref[0, :, 128 * (hd // 2):128 * (hd // 2 + 1)].astype(F32)
        qi_ref[hd * bq:(hd + 1) * bq, :] = jnp.where(keep, pair, 0.0).astype(BF16)
        pair = aq_ref[0, :, 128 * (hd // 2):128 * (hd // 2 + 1)].astype(F32)
        qa_ref[hd // grp, (hd % grp) * bq:(hd % grp + 1) * bq, :] = jnp.where(keep, pair, 0.0).astype(BF16)

    qpos = i * bq + lax.broadcasted_iota(jnp.int32, (1, bq), 1)

    iwt = iwt_ref[0]

    def dots_stage(j, buf):
        kc = ikd_ref[0, pl.ds(pl.multiple_of(j * w, w), w), :]
        for half in range(2):
            buf[half] = lax.dot_general(kc, qi_ref[half * grp * bq:(half + 1) * grp * bq, :], NT_DIMS,
                                        preferred_element_type=F32)
        return ()

    def combine_stage(j, buf, carry, aux):
        start = pl.multiple_of(j * w, w)
        acc = jnp.zeros((w, bq), F32)
        for hd in range(IDX_HEADS):
            x = buf[hd // grp, :, (hd % grp) * bq:(hd % grp + 1) * bq]
            acc = acc + jnp.maximum(x, 0.0) * iwt[hd:hd + 1, :]
        kpos = start + lax.broadcasted_iota(jnp.int32, (w, bq), 0)
        causal = kpos <= qpos
        st_ref[pl.ds(start, w), :] = jnp.where(causal, acc, NEG_INF)
        mn, mx = carry
        lo_s = jnp.where(causal, acc, jnp.inf)
        hi_s = jnp.where(causal, acc, NEG_INF)
        for r in range(w // SUBLANES):
            mn = jnp.minimum(mn, lo_s[r * SUBLANES:(r + 1) * SUBLANES, :])
            mx = jnp.maximum(mx, hi_s[r * SUBLANES:(r + 1) * SUBLANES, :])
        return mn, mx

    mn8, mx8 = _pipelined_steps(n_steps, dots_stage, combine_stage, (xa_ref, xb_ref),
                                (jnp.full((SUBLANES, bq), jnp.inf, F32), jnp.full((SUBLANES, bq), NEG_INF, F32)))
    row_min = jnp.min(mn8, axis=0, keepdims=True)
    row_max = jnp.max(mx8, axis=0, keepdims=True)

    def sweep(fn, init):
        def body(start, j, accs):
            s = st_ref[pl.ds(start, w), :]
            kpos0 = start + lax.broadcasted_iota(jnp.int32, (SUBLANES, bq), 0)
            accs = list(accs)
            for r in range(w // SUBLANES):
                lane_acc = fn(s[r * SUBLANES:(r + 1) * SUBLANES, :], kpos0 + r * SUBLANES, accs[r % 4])
                accs[r % 4] = lane_acc
            return tuple(accs)
        return key_steps(body, (init,) * 4)

    def count_where(pred):
        a = sweep(lambda s, kpos, acc: acc + jnp.where(pred(s, kpos), 1.0, 0.0), jnp.zeros((SUBLANES, bq), F32))
        return jnp.sum((a[0] + a[1]) + (a[2] + a[3]), axis=0, keepdims=True)

    def min_where(pred):
        a = sweep(lambda s, kpos, acc: jnp.minimum(acc, jnp.where(pred(s, kpos), s, jnp.inf)),
                  jnp.full((SUBLANES, bq), jnp.inf, F32))
        return jnp.min(jnp.minimum(jnp.minimum(a[0], a[1]), jnp.minimum(a[2], a[3])), axis=0, keepdims=True)

    def rows8(v):
        return jnp.broadcast_to(v, (SUBLANES, bq))

    kf = float(topk)
    n_causal = (qpos + 1).astype(F32)
    need = n_causal > kf
    all_key = _float_key(jnp.full((1, bq), ALL_KEYS, F32))
    lo0 = jnp.where(need, _float_key(row_min), all_key)
    hi0 = jnp.where(need, _float_key(row_max), all_key)

    def is_active(lo, hi, cnt):
        return (lo < hi) & (cnt > kf + 1.0)

    def bracket_pass(state, use_key_mid):
        lo, hi, cnt = state
        act = is_active(lo, hi, cnt)
        if use_key_mid:
            mid = (lo >> 1) + (hi >> 1) + (((lo & 1) + (hi & 1) + 1) >> 1)
        else:
            fmid = 0.5 * _key_float(lo) + 0.5 * _key_float(hi)
            mid = jnp.minimum(jnp.maximum(_float_key(fmid), lo + 1), hi)
        cb = rows8(_key_float(mid))
        c_mid = count_where(lambda s, kpos: s >= cb)
        ge = c_mid >= kf
        up = act & ge
        dn = act & jnp.logical_not(ge)
        return (jnp.where(up, mid, lo), jnp.where(dn, mid - 1, hi), jnp.where(up, c_mid, cnt))

    def bracket_group(state):
        for p in range(4):
            state = bracket_pass(state, use_key_mid=(p == 3))
        return state

    def any_active(state):
        return jnp.max(is_active(*state).astype(jnp.int32)) > 0

    def key_above(m):
        km = _float_key(m)
        return jnp.where((km >= 0) & (km < 0x00800000), 0x00800000, km + 1)

    def tie_probe(state):
        lo, hi, cnt = state
        act = is_active(lo, hi, cnt)
        lo8 = rows8(_key_float(lo))
        m = min_where(lambda s, kpos: s >= lo8)
        m8 = rows8(m)
        n_gt = count_where(lambda s, kpos: s > m8)
        settled = act & (n_gt < kf)
        step = act & jnp.logical_not(settled)
        km = _float_key(m)
        return (jnp.where(settled, km, jnp.where(step, key_above(m), lo)), jnp.where(settled, km, hi),
                jnp.where(step, n_gt, cnt))

    first_rounds = 4
    state, _ = lax.while_loop(lambda c: any_active(c[0]) & (c[1] < first_rounds),
                              lambda c: (bracket_group(c[0]), c[1] + 1), ((lo0, hi0, n_causal), jnp.int32(0)))
    state = lax.cond(any_active(state), tie_probe, lambda st: st, state)
    lo, hi, cnt = lax.while_loop(any_active, bracket_group, state)
    c_lo = _key_float(lo)
    c_lo8 = rows8(c_lo)
    m_row = min_where(lambda s, kpos: s >= c_lo8)
    c_sel = jnp.where(need & (cnt > kf), _key_float(key_above(m_row)), c_lo)
    c_sel8 = rows8(c_sel)
    n_sel = count_where(lambda s, kpos: s >= c_sel8)
    tie = need & (n_sel != kf)
    c_ref[...] = c_sel

    @pl.when(jnp.max(tie.astype(jnp.int32)) > 0)
    def _():
        want = kf - n_sel
        lower = jnp.where(lax.broadcasted_iota(jnp.int32, (w, w), 0) >= lax.broadcasted_iota(jnp.int32, (w, w), 1),
                          1.0, 0.0).astype(BF16)

        def drop_body(start, j, seen):
            s = st_ref[pl.ds(start, w), :]
            is_m = tie & (s == m_row)
            rank = seen + jnp.dot(lower, jnp.where(is_m, 1.0, 0.0).astype(BF16), preferred_element_type=F32)
            st_ref[pl.ds(start, w), :] = jnp.where(is_m & (rank > want), NEG_INF, s)
            return rank[w - 1:w, :]

        key_steps(drop_body, jnp.zeros((1, bq), F32))
        c_ref[...] = jnp.where(tie, m_row, c_sel)

    acc_ref[...] = jnp.zeros(acc_ref.shape, F32)
    c_fin = c_ref[...]

    def att_scores(j):
        start = pl.multiple_of(j * w, w)
        bias = jnp.where(st_ref[pl.ds(start, w), :] >= c_fin, 0.0, NEG_INF)
        bias4 = jnp.concatenate([bias] * grp, axis=1)
        return [lax.dot_general(akd_ref[0, g, pl.ds(start, w), :], qa_ref[g], NT_DIMS,
                                preferred_element_type=F32) + bias4 for g in range(A_KV_HEADS)]

    _attention_steps(n_steps, att_scores, lambda j, g: avt_ref[0, g, j], acc_ref, (xa_ref, xb_ref), grp * bq)

    blocks = []
    for g in range(A_KV_HEADS):
        blocks += _finish_heads(acc_ref.at[g], grp, bq)
    o_ref[...] = _pack_pairs(blocks).astype(BF16)


def _dsa_attention(iq, iwt, ikd, aq, akd, avt, topk):
    b, lp, _ = iq.shape
    bq = BLOCK_Q
    nq = lp // bq
    w = _key_step(lp)
    grp = A_HEADS // A_KV_HEADS
    return pl.pallas_call(
        functools.partial(_dsa_kernel, bq=bq, w=w, topk=topk),
        grid=(b, nq),
        in_specs=[
            pl.BlockSpec((1, bq, 512), lambda bi, i: (bi, i, 0)),
            pl.BlockSpec((1, IDX_HEADS, bq), lambda bi, i: (bi, 0, i)),
            pl.BlockSpec((1, lp, 128), lambda bi, i: (bi, 0, 0)),
            pl.BlockSpec((1, bq, 512), lambda bi, i: (bi, i, 0)),
            pl.BlockSpec((1, A_KV_HEADS, lp, 128), lambda bi, i: (bi, 0, 0, 0)),
            pl.BlockSpec((1, A_KV_HEADS, lp // w, 128, w), lambda bi, i: (bi, 0, 0, 0, 0)),
        ],
        out_specs=pl.BlockSpec((bq, 512), lambda bi, i: (bi * nq + i, 0)),
        out_shape=jax.ShapeDtypeStruct((b * lp, 512), BF16),
        scratch_shapes=[
            pltpu.VMEM((lp, bq), F32),
            pltpu.VMEM((IDX_HEADS * bq, 128), BF16),
            pltpu.VMEM((A_KV_HEADS, grp * bq, 128), BF16),
            pltpu.VMEM((A_KV_HEADS, 128, grp * bq), F32),
            pltpu.VMEM((1, bq), F32),
            pltpu.VMEM((A_KV_HEADS, w, grp * bq), F32),
            pltpu.VMEM((A_KV_HEADS, w, grp * bq), F32),
        ],
        compiler_params=pltpu.CompilerParams(
            dimension_semantics=("parallel", "arbitrary"), vmem_limit_bytes=VMEM_LIMIT),
        name="dsa_attention",
    )(iq, iwt, ikd, aq, akd, avt)


def _mla_kernel(q_ref, k_ref, vt_ref, o_ref, acc_ref, tri_ref, xa_ref, xb_ref, *, bq):
    i = pl.program_id(2)
    hpg = acc_ref.shape[0]
    acc_ref[...] = jnp.zeros(acc_ref.shape, F32)
    kpos = lax.broadcasted_iota(jnp.int32, (bq, bq), 0)
    qpos = lax.broadcasted_iota(jnp.int32, (bq, bq), 1)
    tri_ref[0] = jnp.zeros((bq, bq), F32)
    tri_ref[1] = jnp.where(kpos <= qpos, 0.0, NEG_INF)

    def scores(j):
        start = pl.multiple_of(j * bq, bq)
        bias = tri_ref[(j == i).astype(jnp.int32)]
        return [lax.dot_general(k_ref[0, hh, pl.ds(start, bq), :], q_ref[0, :, 128 * hh:128 * (hh + 1)],
                                NT_DIMS, preferred_element_type=F32) + bias for hh in range(hpg)]

    _attention_steps(i + 1, scores, lambda j, hh: vt_ref[0, hh, j], acc_ref, (xa_ref, xb_ref), bq)
    blocks = []
    for hh in range(hpg):
        blocks += _finish_heads(acc_ref.at[hh], 1, bq)
    o_ref[...] = _pack_pairs(blocks).astype(BF16)


def _mla_attention(qm, km, vmt):
    b, lp, _ = qm.shape
    bq = _key_step(lp)
    nq = lp // bq
    hpg = 4
    return pl.pallas_call(
        functools.partial(_mla_kernel, bq=bq),
        grid=(b, B_HEADS // hpg, nq),
        in_specs=[
            pl.BlockSpec((1, bq, 128 * hpg), lambda bi, p, i: (bi, i, p)),
            pl.BlockSpec((1, hpg, lp, 128), lambda bi, p, i: (bi, p, 0, 0), pipeline_mode=pl.Buffered(1)),
            pl.BlockSpec((1, hpg, nq, 128, bq), lambda bi, p, i: (bi, p, 0, 0, 0),
                         pipeline_mode=pl.Buffered(1)),
        ],
        out_specs=pl.BlockSpec((bq, B_V * hpg), lambda bi, p, i: (bi * nq + i, p)),
        out_shape=jax.ShapeDtypeStruct((b * lp, B_V * B_HEADS), BF16),
        scratch_shapes=[pltpu.VMEM((hpg, 128, bq), F32), pltpu.VMEM((2, bq, bq), F32),
                        pltpu.VMEM((hpg, bq, bq), F32), pltpu.VMEM((hpg, bq, bq), F32)],
        compiler_params=pltpu.CompilerParams(
            dimension_semantics=("parallel", "parallel", "arbitrary"), vmem_limit_bytes=VMEM_LIMIT),
        name="mla_attention",
    )(qm, km, vmt)


def _merge_kernel(h_ref, oa_ref, ob_ref, sa_ref, sb_ref, wa_ref, wb_ref, wo_ref, g_ref, b_ref, o_ref,
                  *, alpha):
    pa = jnp.dot(oa_ref[...], wa_ref[...], preferred_element_type=F32)
    pb = jnp.dot(ob_ref[...], wb_ref[...], preferred_element_type=F32)
    mixed = sa_ref[...].astype(F32) * pa + sb_ref[...].astype(F32) * pb
    y = alpha * h_ref[...] + jnp.dot(mixed.astype(BF16), wo_ref[...], preferred_element_type=F32)
    o_ref[...] = _layer_norm_rows(y, g_ref[...], b_ref[...])


def _merge_ln(h2d, oa, ob, sa, sb, wa, wb, wo, g, b, alpha):
    t, d = h2d.shape
    tm = _pick_tile(t, (512, 256, 128, 64, 32, 16, 8))
    row = lambda i: (i, 0)
    const = lambda i: (0, 0)
    return pl.pallas_call(
        functools.partial(_merge_kernel, alpha=alpha),
        grid=(t // tm,),
        in_specs=[
            pl.BlockSpec((tm, d), row),
            pl.BlockSpec((tm, oa.shape[1]), row),
            pl.BlockSpec((tm, ob.shape[1]), row),
            pl.BlockSpec((tm, d), row),
            pl.BlockSpec((tm, d), row),
            pl.BlockSpec(wa.shape, const),
            pl.BlockSpec(wb.shape, const),
            pl.BlockSpec(wo.shape, const),
            pl.BlockSpec((1, d), const),
            pl.BlockSpec((1, d), const),
        ],
        out_specs=pl.BlockSpec((tm, d), row),
        out_shape=jax.ShapeDtypeStruct((t, d), F32),
        compiler_params=pltpu.CompilerParams(
            dimension_semantics=("parallel",), vmem_limit_bytes=VMEM_LIMIT),
        name="merge_ln",
    )(h2d, oa, ob, sa, sb, wa, wb, wo, g.reshape(1, d), b.reshape(1, d))


def kernel(x, meta_tokens, ln_g, ln_b, ffn1_w13, ffn1_w2, w_in, mla_q_norm, mla_kv_norm, mla_w_uq,
           mla_w_ukv, w_branch_a, w_branch_b, w_out, ffn2_w13, ffn2_w2):
    b, s, d = x.shape
    depth = ln_g.shape[0]
    alpha = (2 * depth) ** 0.25
    total = s + N_META
    lp = -(-total // BLOCK_Q) * BLOCK_Q
    topk = min(TOPK_MAX, s // 4)
    meta = jnp.broadcast_to(meta_tokens[None].astype(x.dtype), (b, N_META, d))
    h = jnp.concatenate([meta, x, jnp.zeros((b, lp - total, d), x.dtype)], axis=1).reshape(b * lp, d)
    tables = _rope_tables(lp)
    t = b * lp
    for l in range(depth):
        h2 = _ffn_ln(h, ffn1_w13[l].astype(BF16), ffn1_w2[l].astype(BF16),
                     ln_g[l, 0], ln_b[l, 0], alpha)
        wcat, wuq, wukv = _prep_proj_weights(w_in[l], mla_w_uq[l], mla_w_ukv[l])
        (aq, akd, avt, iq, ikd, iw, qm, km, vmt, sa, sb) = _projection(
            h2, b, wcat, wuq, wukv, mla_q_norm[l], mla_kv_norm[l], tables)
        iwt = jnp.swapaxes(iw[:, :, :IDX_HEADS], 1, 2)
        o_a = _dsa_attention(iq, iwt, ikd, aq, akd, avt, topk)
        o_b = _mla_attention(qm, km, vmt)
        h3 = _merge_ln(h2, o_a, o_b, sa, sb,
                       w_branch_a[l].astype(BF16), w_branch_b[l].astype(BF16), w_out[l].astype(BF16),
                       ln_g[l, 1], ln_b[l, 1], alpha)
        h = _ffn_ln(h3, ffn2_w13[l].astype(BF16), ffn2_w2[l].astype(BF16), ln_g[l, 2], ln_b[l, 2], alpha)
    return h.reshape(b, lp, d)[:, N_META:N_META + s]
```

```python
import functools

import jax
import jax.numpy as jnp
from jax import lax
from jax.experimental import pallas as pl
from jax.experimental.pallas import tpu as pltpu

N_META = 16
ROPE_THETA = 500000.0
BLOCK_Q = 128
TOPK_MAX = 256
A_HEADS = 8
A_KV_HEADS = 2
A_HEAD_DIM = 64
A_ROT = A_HEAD_DIM // 4
IDX_HEADS = 8
IDX_DIM = 64
IDX_ROT = IDX_DIM // 4
B_HEADS = 8
B_NOPE = 64
B_ROPE = 32
B_V = 64
B_Q_LORA = 256
B_KV_LORA = 128
assert (IDX_DIM, IDX_ROT) == (A_HEAD_DIM, A_ROT)

LANES = 128
SUBLANES = 8
VMEM_LIMIT = 56 * 1024 * 1024

F32 = jnp.float32
BF16 = jnp.bfloat16
NEG_INF = float("-inf")
M_INIT = -1e30
ALL_KEYS = -3e38
LOG2E = 1.4426950408889634
NT_DIMS = (((1,), (1,)), ((), ()))


def _pick_tile(n, candidates):
    for c in candidates:
        if n % c == 0:
            return c
    return n


def _key_step(lp):
    return _pick_tile(lp, (640, 512, 256, 128))


def _layer_norm_rows(y, g, b):
    mu = jnp.mean(y, axis=-1, keepdims=True)
    d = y - mu
    var = jnp.mean(d * d, axis=-1, keepdims=True)
    return d * lax.rsqrt(var + 1e-5) * g + b


def _ffn_kernel(x_ref, w13_ref, w2_ref, g_ref, b_ref, o_ref, *, alpha, d_ff, chunk):
    x = x_ref[...]
    xb = x.astype(BF16)
    acc = jnp.zeros(x.shape, F32)
    for c in range(d_ff // chunk):
        gate = jnp.dot(xb, w13_ref[:, c * chunk:(c + 1) * chunk], preferred_element_type=F32)
        up = jnp.dot(xb, w13_ref[:, d_ff + c * chunk:d_ff + (c + 1) * chunk],
                     preferred_element_type=F32)
        act = (gate * jax.nn.sigmoid(gate) * up).astype(BF16)
        acc = acc + jnp.dot(act, w2_ref[c * chunk:(c + 1) * chunk, :], preferred_element_type=F32)
    y = alpha * x + 0.5 * acc
    o_ref[...] = _layer_norm_rows(y, g_ref[...], b_ref[...])


def _ffn_ln(h2d, w13, w2, g, b, alpha):
    t, d = h2d.shape
    d_ff = w2.shape[0]
    tm = _pick_tile(t, (512, 256, 128, 64, 32, 16, 8))
    chunk = _pick_tile(d_ff, (1408, 1024, 512, 256, 128))
    const = lambda i: (0, 0)
    return pl.pallas_call(
        functools.partial(_ffn_kernel, alpha=alpha, d_ff=d_ff, chunk=chunk),
        grid=(t // tm,),
        in_specs=[
            pl.BlockSpec((tm, d), lambda i: (i, 0)),
            pl.BlockSpec((d, 2 * d_ff), const, pipeline_mode=pl.Buffered(1)),
            pl.BlockSpec((d_ff, d), const, pipeline_mode=pl.Buffered(1)),
            pl.BlockSpec((1, d), const),
            pl.BlockSpec((1, d), const),
        ],
        out_specs=pl.BlockSpec((tm, d), lambda i: (i, 0)),
        out_shape=jax.ShapeDtypeStruct((t, d), F32),
        compiler_params=pltpu.CompilerParams(
            dimension_semantics=("parallel",), vmem_limit_bytes=VMEM_LIMIT),
        name="ffn_ln",
    )(h2d, w13, w2, g.reshape(1, d), b.reshape(1, d))


_C_AQ = 0
_C_AK = 512
_C_AV = 768
_C_IQ = 896
_C_IK = 1408
_C_IW = 1536
_C_CQ = 1664
_C_CKV = 1920
_C_KR, _C_KRR = 2048, 2176
_C_GA, _C_GB = 2304, 3328
_C_END = 4352


def _store_vt(v, vt_ref, heads):
    lane = lax.broadcasted_iota(jnp.int32, (v.shape[0], 128), 1)
    tail = jnp.where(lane == B_V, 1.0, 0.0)
    for hd in range(heads):
        pair = v[:, 128 * (hd // 2):128 * (hd // 2 + 1)]
        if hd % 2 == 1:
            pair = pltpu.roll(pair, 64, axis=1)
        vt_ref[0, hd, 0] = jnp.transpose(jnp.where(lane < B_V, pair, tail)).astype(BF16)


def _proj_kernel(h_ref, w_ref, cosa_ref, sina_ref, cosm_ref, sinm_ref, qn_ref, kvn_ref,
                 wuq_ref, wukv_ref,
                 aq_ref, akd_ref, avt_ref, iq_ref, ikd_ref, iw_ref, qm_ref, km_ref, vmt_ref,
                 sa_ref, sb_ref, *, mla_scale):
    hb = h_ref[...].astype(BF16)

    def proj(c0, c1):
        return jnp.dot(hb, w_ref[:, c0:c1], preferred_element_type=F32)

    cosa = cosa_ref[...]
    sina = sina_ref[...]

    def rope16(x):
        n = x.shape[1]
        first = (lax.broadcasted_iota(jnp.int32, x.shape, 1) & (A_HEAD_DIM - 1)) < A_ROT // 2
        partner = jnp.where(first, pltpu.roll(x, n - A_ROT // 2, axis=1), pltpu.roll(x, A_ROT // 2, axis=1))
        reps = n // 128
        cos = cosa if reps == 1 else jnp.concatenate([cosa] * reps, axis=1)
        sin = sina if reps == 1 else jnp.concatenate([sina] * reps, axis=1)
        return x * cos + partner * sin

    aq_ref[0] = (rope16(proj(_C_AQ, _C_AQ + 512)) * (A_HEAD_DIM ** -0.5 * LOG2E)).astype(BF16)
    iq_ref[0] = (rope16(proj(_C_IQ, _C_IQ + 512)) * (IDX_DIM ** -0.5)).astype(BF16)
    for g in range(A_KV_HEADS):
        akd_ref[0, g] = rope16(proj(_C_AK + 128 * g, _C_AK + 128 * (g + 1))).astype(BF16)
    _store_vt(proj(_C_AV, _C_AV + 128), avt_ref, A_KV_HEADS)
    ikd_ref[0] = rope16(proj(_C_IK, _C_IK + 128)).astype(BF16)
    iw_ref[0] = proj(_C_IW, _C_IW + 128) * (IDX_HEADS ** -0.5)
    sa_ref[...] = jax.nn.sigmoid(proj(_C_GA, _C_GA + 1024)).astype(BF16)
    sb_ref[...] = jax.nn.sigmoid(proj(_C_GB, _C_GB + 1024)).astype(BF16)

    cosm = cosm_ref[...]
    sinm = sinm_ref[...]
    cq = proj(_C_CQ, _C_CQ + B_Q_LORA)
    cq = cq * lax.rsqrt(jnp.mean(cq * cq, axis=-1, keepdims=True) + 1e-6) * qn_ref[...]
    cqb = cq.astype(BF16)
    for hd in range(B_HEADS):
        q = jnp.dot(cqb, wuq_ref[:, 128 * hd:128 * (hd + 1)], preferred_element_type=F32)
        qr = jnp.dot(cqb, wuq_ref[:, 1024 + 128 * hd:1024 + 128 * (hd + 1)],
                     preferred_element_type=F32)
        qm_ref[0, :, 128 * hd:128 * (hd + 1)] = ((q * cosm + qr * sinm) * mla_scale).astype(BF16)
    ckv = proj(_C_CKV, _C_CKV + B_KV_LORA)
    ckv = ckv * lax.rsqrt(jnp.mean(ckv * ckv, axis=-1, keepdims=True) + 1e-6) * kvn_ref[...]
    ckvb = ckv.astype(BF16)
    kr = proj(_C_KR, _C_KR + 128) * cosm + proj(_C_KRR, _C_KRR + 128) * sinm
    for hd in range(B_HEADS):
        kn = jnp.dot(ckvb, wukv_ref[:, 128 * hd:128 * (hd + 1)], preferred_element_type=F32)
        km_ref[0, hd] = (kn + kr).astype(BF16)
    _store_vt(jnp.dot(ckvb, wukv_ref[:, 1024:1536], preferred_element_type=F32), vmt_ref, B_HEADS)


def _rot_cols(w, heads, dim, rot):
    d_in = w.shape[0]
    w3 = w.reshape(d_in, heads, dim)
    half = rot // 2
    r = jnp.concatenate([-w3[:, :, half:rot], w3[:, :, :half],
                         jnp.zeros((d_in, heads, dim - rot), w.dtype)], axis=-1)
    return r.reshape(d_in, heads * dim)


def _prep_proj_weights(w_in, w_uq, w_ukv):
    d = w_in.shape[0]
    o = 0
    segs = {}
    for name, width in (("aq", 512), ("ak", 128), ("av", 128), ("iq", 512), ("ik", 64), ("iw", 8),
                        ("cq", 256), ("ckv", 128), ("kr", 32), ("ga", 1024), ("gb", 1024)):
        segs[name] = w_in[:, o:o + width]
        o += width
    z = lambda n: jnp.zeros((d, n), w_in.dtype)
    dup = lambda w, g: jnp.concatenate([w[:, 64 * g:64 * (g + 1)]] * 2, axis=1)
    kr_r = _rot_cols(segs["kr"], 1, B_ROPE, B_ROPE)
    cols = [
        segs["aq"],
        dup(segs["ak"], 0), dup(segs["ak"], 1),
        segs["av"],
        segs["iq"],
        segs["ik"], segs["ik"],
        segs["iw"], z(120),
        segs["cq"], segs["ckv"],
        z(64), segs["kr"], z(32), z(64), kr_r, z(32),
        segs["ga"], segs["gb"],
    ]
    wcat = jnp.concatenate(cols, axis=1).astype(BF16)
    assert wcat.shape[1] == _C_END

    lq = w_uq.shape[0]
    uq3 = w_uq.reshape(lq, B_HEADS, B_NOPE + B_ROPE)
    uq_pad = jnp.concatenate([uq3, jnp.zeros((lq, B_HEADS, 32), w_uq.dtype)], axis=-1)
    rope_part = uq3[:, :, B_NOPE:]
    uq_rot = jnp.concatenate([jnp.zeros((lq, B_HEADS, B_NOPE), w_uq.dtype),
                              -rope_part[:, :, B_ROPE // 2:], rope_part[:, :, :B_ROPE // 2],
                              jnp.zeros((lq, B_HEADS, 32), w_uq.dtype)], axis=-1)
    wuq = jnp.concatenate([uq_pad.reshape(lq, 1024), uq_rot.reshape(lq, 1024)], axis=1).astype(BF16)

    lkv = w_ukv.shape[0]
    ukv3 = w_ukv.reshape(lkv, B_HEADS, B_NOPE + B_V)
    kn_pad = jnp.concatenate([ukv3[:, :, :B_NOPE], jnp.zeros((lkv, B_HEADS, 64), w_ukv.dtype)], axis=-1)
    wukv = jnp.concatenate([kn_pad.reshape(lkv, 1024), ukv3[:, :, B_NOPE:].reshape(lkv, 512)],
                           axis=1).astype(BF16)
    return wcat, wuq, wukv


def _rope_tables(lp):
    pos = jnp.arange(lp, dtype=F32)

    def cs(r):
        half = r // 2
        inv = ROPE_THETA ** (-(jnp.arange(half, dtype=F32) * 2.0 / r))
        ang = pos[:, None] * inv[None, :]
        return jnp.cos(ang), jnp.sin(ang)

    ca, sa = cs(A_ROT)
    one = lambda n: jnp.ones((lp, n), F32)
    zero = lambda n: jnp.zeros((lp, n), F32)
    cos64 = jnp.concatenate([ca, ca, one(A_HEAD_DIM - A_ROT)], axis=1)
    sin64 = jnp.concatenate([-sa, sa, zero(A_HEAD_DIM - A_ROT)], axis=1)
    cosa = jnp.concatenate([cos64, cos64], axis=1)
    sina = jnp.concatenate([sin64, sin64], axis=1)
    cm, sm = cs(B_ROPE)
    cosm = jnp.concatenate([one(B_NOPE), cm, cm, one(32)], axis=1)
    sinm = jnp.concatenate([zero(B_NOPE), sm, sm, zero(32)], axis=1)
    return cosa, sina, cosm, sinm


def _projection(h2d, b, wcat, wuq, wukv, qn, kvn, tables):
    t, d = h2d.shape
    lp = t // b
    tm = _key_step(lp)
    cosa, sina, cosm, sinm = tables
    const2 = lambda bi, i: (0, 0)
    row3 = lambda bi, i: (bi, i, 0)
    row2 = lambda bi, i: (bi * (lp // tm) + i, 0)
    tab = pl.BlockSpec((tm, 128), lambda bi, i: (i, 0))
    out_shapes = [
        jax.ShapeDtypeStruct((b, lp, 512), BF16),
        jax.ShapeDtypeStruct((b, A_KV_HEADS, lp, 128), BF16),
        jax.ShapeDtypeStruct((b, A_KV_HEADS, lp // tm, 128, tm), BF16),
        jax.ShapeDtypeStruct((b, lp, 512), BF16),
        jax.ShapeDtypeStruct((b, lp, 128), BF16),
        jax.ShapeDtypeStruct((b, lp, 128), F32),
        jax.ShapeDtypeStruct((b, lp, 1024), BF16),
        jax.ShapeDtypeStruct((b, B_HEADS, lp, 128), BF16),
        jax.ShapeDtypeStruct((b, B_HEADS, lp // tm, 128, tm), BF16),
        jax.ShapeDtypeStruct((t, 1024), BF16),
        jax.ShapeDtypeStruct((t, 1024), BF16),
    ]
    out_specs = [
        pl.BlockSpec((1, tm, 512), row3),
        pl.BlockSpec((1, A_KV_HEADS, tm, 128), lambda bi, i: (bi, 0, i, 0)),
        pl.BlockSpec((1, A_KV_HEADS, 1, 128, tm), lambda bi, i: (bi, 0, i, 0, 0)),
        pl.BlockSpec((1, tm, 512), row3),
        pl.BlockSpec((1, tm, 128), row3),
        pl.BlockSpec((1, tm, 128), row3),
        pl.BlockSpec((1, tm, 1024), row3),
        pl.BlockSpec((1, B_HEADS, tm, 128), lambda bi, i: (bi, 0, i, 0)),
        pl.BlockSpec((1, B_HEADS, 1, 128, tm), lambda bi, i: (bi, 0, i, 0, 0)),
        pl.BlockSpec((tm, 1024), row2),
        pl.BlockSpec((tm, 1024), row2),
    ]
    return pl.pallas_call(
        functools.partial(_proj_kernel, mla_scale=(B_NOPE + B_ROPE) ** -0.5 * LOG2E),
        grid=(b, lp // tm),
        in_specs=[
            pl.BlockSpec((tm, d), row2),
            pl.BlockSpec(wcat.shape, const2, pipeline_mode=pl.Buffered(1)),
            tab, tab, tab, tab,
            pl.BlockSpec((1, B_Q_LORA), const2),
            pl.BlockSpec((1, B_KV_LORA), const2),
            pl.BlockSpec(wuq.shape, const2, pipeline_mode=pl.Buffered(1)),
            pl.BlockSpec(wukv.shape, const2, pipeline_mode=pl.Buffered(1)),
        ],
        out_specs=out_specs,
        out_shape=out_shapes,
        compiler_params=pltpu.CompilerParams(
            dimension_semantics=("parallel", "parallel"), vmem_limit_bytes=VMEM_LIMIT),
        name="in_proj",
    )(h2d, wcat, cosa, sina, cosm, sinm, qn.reshape(1, -1), kvn.reshape(1, -1), wuq, wukv)


def _float_key(x):
    bits = lax.bitcast_convert_type(x, jnp.int32)
    return bits ^ ((bits >> 31) & 0x7FFFFFFF)


def _key_float(k):
    return lax.bitcast_convert_type(k ^ ((k >> 31) & 0x7FFFFFFF), F32)


def _finish_heads(acc_ref_g, n_heads, bq):
    outs = []
    for hh in range(n_heads):
        a = acc_ref_g[:, hh * bq:(hh + 1) * bq]
        o_t = a / a[B_V:B_V + 1, :]
        outs.append(jnp.transpose(o_t))
    return outs


def _pack_pairs(blocks):
    lane = lax.broadcasted_iota(jnp.int32, blocks[0].shape, 1)
    pairs = []
    for j in range(len(blocks) // 2):
        hi = pltpu.roll(blocks[2 * j + 1], 64, axis=1)
        pairs.append(jnp.where(lane < 64, blocks[2 * j], hi))
    return jnp.concatenate(pairs, axis=1) if len(pairs) > 1 else pairs[0]


def _pipelined_steps(n_steps, stage_a, stage_b, buf_refs, carry):
    def pair(t, c):
        carry, aux = c
        j = 2 * t
        aux1 = stage_a(j + 1, buf_refs[1])
        carry = stage_b(j, buf_refs[0], carry, aux)
        aux2 = stage_a(j + 2, buf_refs[0])
        carry = stage_b(j + 1, buf_refs[1], carry, aux1)
        return carry, aux2

    n_pairs = (n_steps - 1) // 2
    carry, aux = lax.fori_loop(0, n_pairs, pair, (carry, stage_a(0, buf_refs[0])))
    j = 2 * n_pairs

    def two_left(carry, aux):
        aux1 = stage_a(j + 1, buf_refs[1])
        carry = stage_b(j, buf_refs[0], carry, aux)
        return stage_b(j + 1, buf_refs[1], carry, aux1)

    def one_left(carry, aux):
        return stage_b(j, buf_refs[0], carry, aux)

    return lax.cond(j + 1 < n_steps, two_left, one_left, carry, aux)


def _attention_steps(n_steps, scores_fn, vt_fn, acc_ref, buf_refs, n_cols):
    ng = acc_ref.shape[0]

    def stage_a(j, buf):
        mx = []
        for g, x in enumerate(scores_fn(j)):
            buf[g] = x
            mx.append(jnp.max(x, axis=0, keepdims=True))
        return tuple(mx)

    def stage_b(j, buf, ms, mx):
        out = []
        for g in range(ng):
            m_new = jnp.maximum(ms[g], mx[g])
            p = jnp.exp2(buf[g] - m_new).astype(BF16)
            alpha = jnp.exp2(ms[g] - m_new)
            acc_ref[g] = acc_ref[g] * alpha + jnp.dot(vt_fn(j, g), p, preferred_element_type=F32)
            out.append(m_new)
        return tuple(out)

    m0 = jnp.full((1, n_cols), M_INIT, F32)
    _pipelined_steps(n_steps, stage_a, stage_b, buf_refs, (m0,) * ng)


def _dsa_kernel(iq_ref, iwt_ref, ikd_ref, aq_ref, akd_ref, avt_ref, o_ref,
                st_ref, qi_ref, qa_ref, acc_ref, c_ref, xa_ref, xb_ref, *, bq, w, topk):
    i = pl.program_id(1)
    n_steps = ((i + 1) * bq + (w - 1)) // w
    grp = A_HEADS // A_KV_HEADS

    def key_steps(body, init):
        return lax.fori_loop(0, n_steps, lambda j, c: body(pl.multiple_of(j * w, w), j, c), init)

    lane = lax.broadcasted_iota(jnp.int32, (bq, 128), 1)
    for hd in range(IDX_HEADS):
        keep = (lane < 64) if hd % 2 == 0 else (lane >= 64)
        pair = iq_ref[0, :, 128 * (hd // 2):128 * (hd // 2 + 1)].astype(F32)
        qi_ref[hd * bq:(hd + 1) * bq, :] = jnp.where(keep, pair, 0.0).astype(BF16)
        pair = aq_ref[0, :, 128 * (hd // 2):128 * (hd // 2 + 1)].astype(F32)
        qa_ref[hd // grp, (hd % grp) * bq:(hd % grp + 1) * bq, :] = jnp.where(keep, pair, 0.0).astype(BF16)

    qpos = i * bq + lax.broadcasted_iota(jnp.int32, (1, bq), 1)

    iwt = iwt_ref[0]

    def dots_stage(j, buf):
        kc = ikd_ref[0, pl.ds(pl.multiple_of(j * w, w), w), :]
        for half in range(2):
            buf[half] = lax.dot_general(kc, qi_ref[half * grp * bq:(half + 1) * grp * bq, :], NT_DIMS,
                                        preferred_element_type=F32)
        return ()

    def combine_stage(j, buf, carry, aux):
        start = pl.multiple_of(j * w, w)
        acc = jnp.zeros((w, bq), F32)
        for hd in range(IDX_HEADS):
            x = buf[hd // grp, :, (hd % grp) * bq:(hd % grp + 1) * bq]
            acc = acc + jnp.maximum(x, 0.0) * iwt[hd:hd + 1, :]
        kpos = start + lax.broadcasted_iota(jnp.int32, (w, bq), 0)
        causal = kpos <= qpos
        st_ref[pl.ds(start, w), :] = jnp.where(causal, acc, NEG_INF)
        mn, mx = carry
        lo_s = jnp.where(causal, acc, jnp.inf)
        hi_s = jnp.where(causal, acc, NEG_INF)
        for r in range(w // SUBLANES):
            mn = jnp.minimum(mn, lo_s[r * SUBLANES:(r + 1) * SUBLANES, :])
            mx = jnp.maximum(mx, hi_s[r * SUBLANES:(r + 1) * SUBLANES, :])
        return mn, mx

    mn8, mx8 = _pipelined_steps(n_steps, dots_stage, combine_stage, (xa_ref, xb_ref),
                                (jnp.full((SUBLANES, bq), jnp.inf, F32), jnp.full((SUBLANES, bq), NEG_INF, F32)))
    row_min = jnp.min(mn8, axis=0, keepdims=True)
    row_max = jnp.max(mx8, axis=0, keepdims=True)

    def sweep(fn, init):
        def body(start, j, accs):
            s = st_ref[pl.ds(start, w), :]
            kpos0 = start + lax.broadcasted_iota(jnp.int32, (SUBLANES, bq), 0)
            accs = list(accs)
            for r in range(w // SUBLANES):
                lane_acc = fn(s[r * SUBLANES:(r + 1) * SUBLANES, :], kpos0 + r * SUBLANES, accs[r % 4])
                accs[r % 4] = lane_acc
            return tuple(accs)
        return key_steps(body, (init,) * 4)

    def count_where(pred):
        a = sweep(lambda s, kpos, acc: acc + jnp.where(pred(s, kpos), 1.0, 0.0), jnp.zeros((SUBLANES, bq), F32))
        return jnp.sum((a[0] + a[1]) + (a[2] + a[3]), axis=0, keepdims=True)

    def min_where(pred):
        a = sweep(lambda s, kpos, acc: jnp.minimum(acc, jnp.where(pred(s, kpos), s, jnp.inf)),
                  jnp.full((SUBLANES, bq), jnp.inf, F32))
        return jnp.min(jnp.minimum(jnp.minimum(a[0], a[1]), jnp.minimum(a[2], a[3])), axis=0, keepdims=True)

    def rows8(v):
        return jnp.broadcast_to(v, (SUBLANES, bq))

    kf = float(topk)
    n_causal = (qpos + 1).astype(F32)
    need = n_causal > kf
    all_key = _float_key(jnp.full((1, bq), ALL_KEYS, F32))
    lo0 = jnp.where(need, _float_key(row_min), all_key)
    hi0 = jnp.where(need, _float_key(row_max), all_key)

    def is_active(lo, hi, cnt):
        return (lo < hi) & (cnt > kf + 1.0)

    def bracket_pass(state, use_key_mid):
        lo, hi, cnt = state
        act = is_active(lo, hi, cnt)
        if use_key_mid:
            mid = (lo >> 1) + (hi >> 1) + (((lo & 1) + (hi & 1) + 1) >> 1)
        else:
            fmid = 0.5 * _key_float(lo) + 0.5 * _key_float(hi)
            mid = jnp.minimum(jnp.maximum(_float_key(fmid), lo + 1), hi)
        cb = rows8(_key_float(mid))
        c_mid = count_where(lambda s, kpos: s >= cb)
        ge = c_mid >= kf
        up = act & ge
        dn = act & jnp.logical_not(ge)
        return (jnp.where(up, mid, lo), jnp.where(dn, mid - 1, hi), jnp.where(up, c_mid, cnt))

    def bracket_group(state):
        for p in range(4):
            state = bracket_pass(state, use_key_mid=(p == 3))
        return state

    def any_active(state):
        return jnp.max(is_active(*state).astype(jnp.int32)) > 0

    def key_above(m):
        km = _float_key(m)
        return jnp.where((km >= 0) & (km < 0x00800000), 0x00800000, km + 1)

    def tie_probe(state):
        lo, hi, cnt = state
        act = is_active(lo, hi, cnt)
        lo8 = rows8(_key_float(lo))
        m = min_where(lambda s, kpos: s >= lo8)
        m8 = rows8(m)
        n_gt = count_where(lambda s, kpos: s > m8)
        settled = act & (n_gt < kf)
        step = act & jnp.logical_not(settled)
        km = _float_key(m)
        return (jnp.where(settled, km, jnp.where(step, key_above(m), lo)), jnp.where(settled, km, hi),
                jnp.where(step, n_gt, cnt))

    first_rounds = 4
    state, _ = lax.while_loop(lambda c: any_active(c[0]) & (c[1] < first_rounds),
                              lambda c: (bracket_group(c[0]), c[1] + 1), ((lo0, hi0, n_causal), jnp.int32(0)))
    max_probes = 2
    state, _ = lax.while_loop(lambda c: any_active(c[0]) & (c[1] < max_probes),
                              lambda c: (tie_probe(c[0]), c[1] + 1), (state, jnp.int32(0)))
    lo, hi, cnt = lax.while_loop(any_active, bracket_group, state)
    c_lo = _key_float(lo)
    c_lo8 = rows8(c_lo)
    m_row = min_where(lambda s, kpos: s >= c_lo8)
    c_sel = jnp.where(need & (cnt > kf), _key_float(key_above(m_row)), c_lo)
    c_sel8 = rows8(c_sel)
    n_sel = count_where(lambda s, kpos: s >= c_sel8)
    tie = need & (n_sel != kf)
    c_ref[...] = c_sel

    @pl.when(jnp.max(tie.astype(jnp.int32)) > 0)
    def _():
        want = kf - n_sel
        sub = 128
        lower = jnp.where(lax.broadcasted_iota(jnp.int32, (sub, sub), 0)
                          >= lax.broadcasted_iota(jnp.int32, (sub, sub), 1), 1.0, 0.0).astype(BF16)

        def drop_body(start, j, seen):
            for r in range(w // sub):
                s = st_ref[pl.ds(start + r * sub, sub), :]
                is_m = tie & (s == m_row)
                rank = seen + jnp.dot(lower, jnp.where(is_m, 1.0, 0.0).astype(BF16), preferred_element_type=F32)
                st_ref[pl.ds(start + r * sub, sub), :] = jnp.where(is_m & (rank > want), NEG_INF, s)
                seen = rank[sub - 1:sub, :]
            return seen

        key_steps(drop_body, jnp.zeros((1, bq), F32))
        c_ref[...] = jnp.where(tie, m_row, c_sel)

    acc_ref[...] = jnp.zeros(acc_ref.shape, F32)
    c_fin = c_ref[...]

    def att_scores(j):
        start = pl.multiple_of(j * w, w)
        bias = jnp.where(st_ref[pl.ds(start, w), :] >= c_fin, 0.0, NEG_INF)
        bias4 = jnp.concatenate([bias] * grp, axis=1)
        return [lax.dot_general(akd_ref[0, g, pl.ds(start, w), :], qa_ref[g], NT_DIMS,
                                preferred_element_type=F32) + bias4 for g in range(A_KV_HEADS)]

    _attention_steps(n_steps, att_scores, lambda j, g: avt_ref[0, g, j], acc_ref, (xa_ref, xb_ref), grp * bq)

    blocks = []
    for g in range(A_KV_HEADS):
        blocks += _finish_heads(acc_ref.at[g], grp, bq)
    o_ref[...] = _pack_pairs(blocks).astype(BF16)


def _dsa_attention(iq, iwt, ikd, aq, akd, avt, topk):
    b, lp, _ = iq.shape
    bq = BLOCK_Q
    nq = lp // bq
    w = _key_step(lp)
    grp = A_HEADS // A_KV_HEADS
    return pl.pallas_call(
        functools.partial(_dsa_kernel, bq=bq, w=w, topk=topk),
        grid=(b, nq),
        in_specs=[
            pl.BlockSpec((1, bq, 512), lambda bi, i: (bi, i, 0)),
            pl.BlockSpec((1, IDX_HEADS, bq), lambda bi, i: (bi, 0, i)),
            pl.BlockSpec((1, lp, 128), lambda bi, i: (bi, 0, 0)),
            pl.BlockSpec((1, bq, 512), lambda bi, i: (bi, i, 0)),
            pl.BlockSpec((1, A_KV_HEADS, lp, 128), lambda bi, i: (bi, 0, 0, 0)),
            pl.BlockSpec((1, A_KV_HEADS, lp // w, 128, w), lambda bi, i: (bi, 0, 0, 0, 0)),
        ],
        out_specs=pl.BlockSpec((bq, 512), lambda bi, i: (bi * nq + i, 0)),
        out_shape=jax.ShapeDtypeStruct((b * lp, 512), BF16),
        scratch_shapes=[
            pltpu.VMEM((lp, bq), F32),
            pltpu.VMEM((IDX_HEADS * bq, 128), BF16),
            pltpu.VMEM((A_KV_HEADS, grp * bq, 128), BF16),
            pltpu.VMEM((A_KV_HEADS, 128, grp * bq), F32),
            pltpu.VMEM((1, bq), F32),
            pltpu.VMEM((A_KV_HEADS, w, grp * bq), F32),
            pltpu.VMEM((A_KV_HEADS, w, grp * bq), F32),
        ],
        compiler_params=pltpu.CompilerParams(
            dimension_semantics=("parallel", "arbitrary"), vmem_limit_bytes=VMEM_LIMIT),
        name="dsa_attention",
    )(iq, iwt, ikd, aq, akd, avt)


def _mla_kernel(q_ref, k_ref, vt_ref, o_ref, acc_ref, tri_ref, xa_ref, xb_ref, *, bq):
    i = pl.program_id(2)
    hpg = acc_ref.shape[0]
    acc_ref[...] = jnp.zeros(acc_ref.shape, F32)
    kpos = lax.broadcasted_iota(jnp.int32, (bq, bq), 0)
    qpos = lax.broadcasted_iota(jnp.int32, (bq, bq), 1)
    tri_ref[0] = jnp.zeros((bq, bq), F32)
    tri_ref[1] = jnp.where(kpos <= qpos, 0.0, NEG_INF)

    def scores(j):
        start = pl.multiple_of(j * bq, bq)
        bias = tri_ref[(j == i).astype(jnp.int32)]
        return [lax.dot_general(k_ref[0, hh, pl.ds(start, bq), :], q_ref[0, :, 128 * hh:128 * (hh + 1)],
                                NT_DIMS, preferred_element_type=F32) + bias for hh in range(hpg)]

    _attention_steps(i + 1, scores, lambda j, hh: vt_ref[0, hh, j], acc_ref, (xa_ref, xb_ref), bq)
    blocks = []
    for hh in range(hpg):
        blocks += _finish_heads(acc_ref.at[hh], 1, bq)
    o_ref[...] = _pack_pairs(blocks).astype(BF16)


def _mla_attention(qm, km, vmt):
    b, lp, _ = qm.shape
    bq = _key_step(lp)
    nq = lp // bq
    hpg = 4
    return pl.pallas_call(
        functools.partial(_mla_kernel, bq=bq),
        grid=(b, B_HEADS // hpg, nq),
        in_specs=[
            pl.BlockSpec((1, bq, 128 * hpg), lambda bi, p, i: (bi, i, p)),
            pl.BlockSpec((1, hpg, lp, 128), lambda bi, p, i: (bi, p, 0, 0), pipeline_mode=pl.Buffered(1)),
            pl.BlockSpec((1, hpg, nq, 128, bq), lambda bi, p, i: (bi, p, 0, 0, 0),
                         pipeline_mode=pl.Buffered(1)),
        ],
        out_specs=pl.BlockSpec((bq, B_V * hpg), lambda bi, p, i: (bi * nq + i, p)),
        out_shape=jax.ShapeDtypeStruct((b * lp, B_V * B_HEADS), BF16),
        scratch_shapes=[pltpu.VMEM((hpg, 128, bq), F32), pltpu.VMEM((2, bq, bq), F32),
                        pltpu.VMEM((hpg, bq, bq), F32), pltpu.VMEM((hpg, bq, bq), F32)],
        compiler_params=pltpu.CompilerParams(
            dimension_semantics=("parallel", "parallel", "arbitrary"), vmem_limit_bytes=VMEM_LIMIT),
        name="mla_attention",
    )(qm, km, vmt)


def _merge_kernel(h_ref, oa_ref, ob_ref, sa_ref, sb_ref, wa_ref, wb_ref, wo_ref, g_ref, b_ref, o_ref,
                  *, alpha):
    pa = jnp.dot(oa_ref[...], wa_ref[...], preferred_element_type=F32)
    pb = jnp.dot(ob_ref[...], wb_ref[...], preferred_element_type=F32)
    mixed = sa_ref[...].astype(F32) * pa + sb_ref[...].astype(F32) * pb
    y = alpha * h_ref[...] + jnp.dot(mixed.astype(BF16), wo_ref[...], preferred_element_type=F32)
    o_ref[...] = _layer_norm_rows(y, g_ref[...], b_ref[...])


def _merge_ln(h2d, oa, ob, sa, sb, wa, wb, wo, g, b, alpha):
    t, d = h2d.shape
    tm = _pick_tile(t, (512, 256, 128, 64, 32, 16, 8))
    row = lambda i: (i, 0)
    const = lambda i: (0, 0)
    return pl.pallas_call(
        functools.partial(_merge_kernel, alpha=alpha),
        grid=(t // tm,),
        in_specs=[
            pl.BlockSpec((tm, d), row),
            pl.BlockSpec((tm, oa.shape[1]), row),
            pl.BlockSpec((tm, ob.shape[1]), row),
            pl.BlockSpec((tm, d), row),
            pl.BlockSpec((tm, d), row),
            pl.BlockSpec(wa.shape, const),
            pl.BlockSpec(wb.shape, const),
            pl.BlockSpec(wo.shape, const),
            pl.BlockSpec((1, d), const),
            pl.BlockSpec((1, d), const),
        ],
        out_specs=pl.BlockSpec((tm, d), row),
        out_shape=jax.ShapeDtypeStruct((t, d), F32),
        compiler_params=pltpu.CompilerParams(
            dimension_semantics=("parallel",), vmem_limit_bytes=VMEM_LIMIT),
        name="merge_ln",
    )(h2d, oa, ob, sa, sb, wa, wb, wo, g.reshape(1, d), b.reshape(1, d))


def kernel(x, meta_tokens, ln_g, ln_b, ffn1_w13, ffn1_w2, w_in, mla_q_norm, mla_kv_norm, mla_w_uq,
           mla_w_ukv, w_branch_a, w_branch_b, w_out, ffn2_w13, ffn2_w2):
    b, s, d = x.shape
    depth = ln_g.shape[0]
    alpha = (2 * depth) ** 0.25
    total = s + N_META
    lp = -(-total // BLOCK_Q) * BLOCK_Q
    topk = min(TOPK_MAX, s // 4)
    meta = jnp.broadcast_to(meta_tokens[None].astype(x.dtype), (b, N_META, d))
    h = jnp.concatenate([meta, x, jnp.zeros((b, lp - total, d), x.dtype)], axis=1).reshape(b * lp, d)
    tables = _rope_tables(lp)
    t = b * lp
    for l in range(depth):
        h2 = _ffn_ln(h, ffn1_w13[l].astype(BF16), ffn1_w2[l].astype(BF16),
                     ln_g[l, 0], ln_b[l, 0], alpha)
        wcat, wuq, wukv = _prep_proj_weights(w_in[l], mla_w_uq[l], mla_w_ukv[l])
        (aq, akd, avt, iq, ikd, iw, qm, km, vmt, sa, sb) = _projection(
            h2, b, wcat, wuq, wukv, mla_q_norm[l], mla_kv_norm[l], tables)
        iwt = jnp.swapaxes(iw[:, :, :IDX_HEADS], 1, 2)
        o_a = _dsa_attention(iq, iwt, ikd, aq, akd, avt, topk)
        o_b = _mla_attention(qm, km, vmt)
        h3 = _merge_ln(h2, o_a, o_b, sa, sb,
                       w_branch_a[l].astype(BF16), w_branch_b[l].astype(BF16), w_out[l].astype(BF16),
                       ln_g[l, 1], ln_b[l, 1], alpha)
        h = _ffn_ln(h3, ffn2_w13[l].astype(BF16), ffn2_w2[l].astype(BF16), ln_g[l, 2], ln_b[l, 2], alpha)
    return h.reshape(b, lp, d)[:, N_META:N_META + s]
```

```python
import functools

import jax
import jax.numpy as jnp
from jax import lax
from jax.experimental import pallas as pl
from jax.experimental.pallas import tpu as pltpu

N_META = 16
ROPE_THETA = 500000.0
BLOCK_Q = 128
TOPK_MAX = 256
A_HEADS = 8
A_KV_HEADS = 2
A_HEAD_DIM = 64
A_ROT = A_HEAD_DIM // 4
IDX_HEADS = 8
IDX_DIM = 64
IDX_ROT = IDX_DIM // 4
B_HEADS = 8
B_NOPE = 64
B_ROPE = 32
B_V = 64
B_Q_LORA = 256
B_KV_LORA = 128
assert (IDX_DIM, IDX_ROT) == (A_HEAD_DIM, A_ROT)

LANES = 128
SUBLANES = 8
VMEM_LIMIT = 56 * 1024 * 1024

F32 = jnp.float32
BF16 = jnp.bfloat16
NEG_INF = float("-inf")
M_INIT = -1e30
ALL_KEYS = -3e38
LOG2E = 1.4426950408889634
NT_DIMS = (((1,), (1,)), ((), ()))


def _pick_tile(n, candidates):
    for c in candidates:
        if n % c == 0:
            return c
    return n


def _key_step(lp):
    return _pick_tile(lp, (640, 512, 256, 128))


def _layer_norm_rows(y, g, b):
    mu = jnp.mean(y, axis=-1, keepdims=True)
    d = y - mu
    var = jnp.mean(d * d, axis=-1, keepdims=True)
    return d * lax.rsqrt(var + 1e-5) * g + b


def _ffn_kernel(x_ref, w13_ref, w2_ref, g_ref, b_ref, o_ref, *, alpha, d_ff, chunk):
    x = x_ref[...]
    xb = x.astype(BF16)
    acc = jnp.zeros(x.shape, F32)
    for c in range(d_ff // chunk):
        gate = jnp.dot(xb, w13_ref[:, c * chunk:(c + 1) * chunk], preferred_element_type=F32)
        up = jnp.dot(xb, w13_ref[:, d_ff + c * chunk:d_ff + (c + 1) * chunk],
                     preferred_element_type=F32)
        act = (gate * jax.nn.sigmoid(gate) * up).astype(BF16)
        acc = acc + jnp.dot(act, w2_ref[c * chunk:(c + 1) * chunk, :], preferred_element_type=F32)
    y = alpha * x + 0.5 * acc
    o_ref[...] = _layer_norm_rows(y, g_ref[...], b_ref[...])


def _ffn_ln(h2d, w13, w2, g, b, alpha):
    t, d = h2d.shape
    d_ff = w2.shape[0]
    tm = _pick_tile(t, (512, 256, 128, 64, 32, 16, 8))
    chunk = _pick_tile(d_ff, (1408, 1024, 512, 256, 128))
    const = lambda i: (0, 0)
    return pl.pallas_call(
        functools.partial(_ffn_kernel, alpha=alpha, d_ff=d_ff, chunk=chunk),
        grid=(t // tm,),
        in_specs=[
            pl.BlockSpec((tm, d), lambda i: (i, 0)),
            pl.BlockSpec((d, 2 * d_ff), const, pipeline_mode=pl.Buffered(1)),
            pl.BlockSpec((d_ff, d), const, pipeline_mode=pl.Buffered(1)),
            pl.BlockSpec((1, d), const),
            pl.BlockSpec((1, d), const),
        ],
        out_specs=pl.BlockSpec((tm, d), lambda i: (i, 0)),
        out_shape=jax.ShapeDtypeStruct((t, d), F32),
        compiler_params=pltpu.CompilerParams(
            dimension_semantics=("parallel",), vmem_limit_bytes=VMEM_LIMIT),
        name="ffn_ln",
    )(h2d, w13, w2, g.reshape(1, d), b.reshape(1, d))


_C_AQ = 0
_C_AK = 512
_C_AV = 768
_C_IQ = 896
_C_IK = 1408
_C_IW = 1536
_C_CQ = 1664
_C_CKV = 1920
_C_KR, _C_KRR = 2048, 2176
_C_GA, _C_GB = 2304, 3328
_C_END = 4352


def _store_vt(v, vt_ref, heads):
    lane = lax.broadcasted_iota(jnp.int32, (v.shape[0], 128), 1)
    tail = jnp.where(lane == B_V, 1.0, 0.0)
    for hd in range(heads):
        pair = v[:, 128 * (hd // 2):128 * (hd // 2 + 1)]
        if hd % 2 == 1:
            pair = pltpu.roll(pair, 64, axis=1)
        vt_ref[0, hd, 0] = jnp.transpose(jnp.where(lane < B_V, pair, tail)).astype(BF16)


def _proj_kernel(h_ref, w_ref, cosa_ref, sina_ref, cosm_ref, sinm_ref, qn_ref, kvn_ref,
                 wuq_ref, wukv_ref,
                 aq_ref, akd_ref, avt_ref, iq_ref, ikd_ref, iw_ref, qm_ref, km_ref, vmt_ref,
                 sa_ref, sb_ref, *, mla_scale):
    hb = h_ref[...].astype(BF16)

    def proj(c0, c1):
        return jnp.dot(hb, w_ref[:, c0:c1], preferred_element_type=F32)

    cosa = cosa_ref[...]
    sina = sina_ref[...]

    def rope16(x):
        n = x.shape[1]
        first = (lax.broadcasted_iota(jnp.int32, x.shape, 1) & (A_HEAD_DIM - 1)) < A_ROT // 2
        partner = jnp.where(first, pltpu.roll(x, n - A_ROT // 2, axis=1), pltpu.roll(x, A_ROT // 2, axis=1))
        reps = n // 128
        cos = cosa if reps == 1 else jnp.concatenate([cosa] * reps, axis=1)
        sin = sina if reps == 1 else jnp.concatenate([sina] * reps, axis=1)
        return x * cos + partner * sin

    aq_ref[0] = (rope16(proj(_C_AQ, _C_AQ + 512)) * (A_HEAD_DIM ** -0.5 * LOG2E)).astype(BF16)
    iq_ref[0] = (rope16(proj(_C_IQ, _C_IQ + 512)) * (IDX_DIM ** -0.5)).astype(BF16)
    for g in range(A_KV_HEADS):
        akd_ref[0, g] = rope16(proj(_C_AK + 128 * g, _C_AK + 128 * (g + 1))).astype(BF16)
    _store_vt(proj(_C_AV, _C_AV + 128), avt_ref, A_KV_HEADS)
    ikd_ref[0] = rope16(proj(_C_IK, _C_IK + 128)).astype(BF16)
    iw_ref[0] = proj(_C_IW, _C_IW + 128) * (IDX_HEADS ** -0.5)
    sa_ref[...] = jax.nn.sigmoid(proj(_C_GA, _C_GA + 1024)).astype(BF16)
    sb_ref[...] = jax.nn.sigmoid(proj(_C_GB, _C_GB + 1024)).astype(BF16)

    cosm = cosm_ref[...]
    sinm = sinm_ref[...]
    cq = proj(_C_CQ, _C_CQ + B_Q_LORA)
    cq = cq * lax.rsqrt(jnp.mean(cq * cq, axis=-1, keepdims=True) + 1e-6) * qn_ref[...]
    cqb = cq.astype(BF16)
    for hd in range(B_HEADS):
        q = jnp.dot(cqb, wuq_ref[:, 128 * hd:128 * (hd + 1)], preferred_element_type=F32)
        qr = jnp.dot(cqb, wuq_ref[:, 1024 + 128 * hd:1024 + 128 * (hd + 1)],
                     preferred_element_type=F32)
        qm_ref[0, :, 128 * hd:128 * (hd + 1)] = ((q * cosm + qr * sinm) * mla_scale).astype(BF16)
    ckv = proj(_C_CKV, _C_CKV + B_KV_LORA)
    ckv = ckv * lax.rsqrt(jnp.mean(ckv * ckv, axis=-1, keepdims=True) + 1e-6) * kvn_ref[...]
    ckvb = ckv.astype(BF16)
    kr = proj(_C_KR, _C_KR + 128) * cosm + proj(_C_KRR, _C_KRR + 128) * sinm
    for hd in range(B_HEADS):
        kn = jnp.dot(ckvb, wukv_ref[:, 128 * hd:128 * (hd + 1)], preferred_element_type=F32)
        km_ref[0, hd] = (kn + kr).astype(BF16)
    _store_vt(jnp.dot(ckvb, wukv_ref[:, 1024:1536], preferred_element_type=F32), vmt_ref, B_HEADS)


def _rot_cols(w, heads, dim, rot):
    d_in = w.shape[0]
    w3 = w.reshape(d_in, heads, dim)
    half = rot // 2
    r = jnp.concatenate([-w3[:, :, half:rot], w3[:, :, :half],
                         jnp.zeros((d_in, heads, dim - rot), w.dtype)], axis=-1)
    return r.reshape(d_in, heads * dim)


def _prep_proj_weights(w_in, w_uq, w_ukv):
    d = w_in.shape[0]
    o = 0
    segs = {}
    for name, width in (("aq", 512), ("ak", 128), ("av", 128), ("iq", 512), ("ik", 64), ("iw", 8),
                        ("cq", 256), ("ckv", 128), ("kr", 32), ("ga", 1024), ("gb", 1024)):
        segs[name] = w_in[:, o:o + width]
        o += width
    z = lambda n: jnp.zeros((d, n), w_in.dtype)
    dup = lambda w, g: jnp.concatenate([w[:, 64 * g:64 * (g + 1)]] * 2, axis=1)
    kr_r = _rot_cols(segs["kr"], 1, B_ROPE, B_ROPE)
    cols = [
        segs["aq"],
        dup(segs["ak"], 0), dup(segs["ak"], 1),
        segs["av"],
        segs["iq"],
        segs["ik"], segs["ik"],
        segs["iw"], z(120),
        segs["cq"], segs["ckv"],
        z(64), segs["kr"], z(32), z(64), kr_r, z(32),
        segs["ga"], segs["gb"],
    ]
    wcat = jnp.concatenate(cols, axis=1).astype(BF16)
    assert wcat.shape[1] == _C_END

    lq = w_uq.shape[0]
    uq3 = w_uq.reshape(lq, B_HEADS, B_NOPE + B_ROPE)
    uq_pad = jnp.concatenate([uq3, jnp.zeros((lq, B_HEADS, 32), w_uq.dtype)], axis=-1)
    rope_part = uq3[:, :, B_NOPE:]
    uq_rot = jnp.concatenate([jnp.zeros((lq, B_HEADS, B_NOPE), w_uq.dtype),
                              -rope_part[:, :, B_ROPE // 2:], rope_part[:, :, :B_ROPE // 2],
                              jnp.zeros((lq, B_HEADS, 32), w_uq.dtype)], axis=-1)
    wuq = jnp.concatenate([uq_pad.reshape(lq, 1024), uq_rot.reshape(lq, 1024)], axis=1).astype(BF16)

    lkv = w_ukv.shape[0]
    ukv3 = w_ukv.reshape(lkv, B_HEADS, B_NOPE + B_V)
    kn_pad = jnp.concatenate([ukv3[:, :, :B_NOPE], jnp.zeros((lkv, B_HEADS, 64), w_ukv.dtype)], axis=-1)
    wukv = jnp.concatenate([kn_pad.reshape(lkv, 1024), ukv3[:, :, B_NOPE:].reshape(lkv, 512)],
                           axis=1).astype(BF16)
    return wcat, wuq, wukv


def _rope_tables(lp):
    pos = jnp.arange(lp, dtype=F32)

    def cs(r):
        half = r // 2
        inv = ROPE_THETA ** (-(jnp.arange(half, dtype=F32) * 2.0 / r))
        ang = pos[:, None] * inv[None, :]
        return jnp.cos(ang), jnp.sin(ang)

    ca, sa = cs(A_ROT)
    one = lambda n: jnp.ones((lp, n), F32)
    zero = lambda n: jnp.zeros((lp, n), F32)
    cos64 = jnp.concatenate([ca, ca, one(A_HEAD_DIM - A_ROT)], axis=1)
    sin64 = jnp.concatenate([-sa, sa, zero(A_HEAD_DIM - A_ROT)], axis=1)
    cosa = jnp.concatenate([cos64, cos64], axis=1)
    sina = jnp.concatenate([sin64, sin64], axis=1)
    cm, sm = cs(B_ROPE)
    cosm = jnp.concatenate([one(B_NOPE), cm, cm, one(32)], axis=1)
    sinm = jnp.concatenate([zero(B_NOPE), sm, sm, zero(32)], axis=1)
    return cosa, sina, cosm, sinm


def _projection(h2d, b, wcat, wuq, wukv, qn, kvn, tables):
    t, d = h2d.shape
    lp = t // b
    tm = _key_step(lp)
    cosa, sina, cosm, sinm = tables
    const2 = lambda bi, i: (0, 0)
    row3 = lambda bi, i: (bi, i, 0)
    row2 = lambda bi, i: (bi * (lp // tm) + i, 0)
    tab = pl.BlockSpec((tm, 128), lambda bi, i: (i, 0))
    out_shapes = [
        jax.ShapeDtypeStruct((b, lp, 512), BF16),
        jax.ShapeDtypeStruct((b, A_KV_HEADS, lp, 128), BF16),
        jax.ShapeDtypeStruct((b, A_KV_HEADS, lp // tm, 128, tm), BF16),
        jax.ShapeDtypeStruct((b, lp, 512), BF16),
        jax.ShapeDtypeStruct((b, lp, 128), BF16),
        jax.ShapeDtypeStruct((b, lp, 128), F32),
        jax.ShapeDtypeStruct((b, lp, 1024), BF16),
        jax.ShapeDtypeStruct((b, B_HEADS, lp, 128), BF16),
        jax.ShapeDtypeStruct((b, B_HEADS, lp // tm, 128, tm), BF16),
        jax.ShapeDtypeStruct((t, 1024), BF16),
        jax.ShapeDtypeStruct((t, 1024), BF16),
    ]
    out_specs = [
        pl.BlockSpec((1, tm, 512), row3),
        pl.BlockSpec((1, A_KV_HEADS, tm, 128), lambda bi, i: (bi, 0, i, 0)),
        pl.BlockSpec((1, A_KV_HEADS, 1, 128, tm), lambda bi, i: (bi, 0, i, 0, 0)),
        pl.BlockSpec((1, tm, 512), row3),
        pl.BlockSpec((1, tm, 128), row3),
        pl.BlockSpec((1, tm, 128), row3),
        pl.BlockSpec((1, tm, 1024), row3),
        pl.BlockSpec((1, B_HEADS, tm, 128), lambda bi, i: (bi, 0, i, 0)),
        pl.BlockSpec((1, B_HEADS, 1, 128, tm), lambda bi, i: (bi, 0, i, 0, 0)),
        pl.BlockSpec((tm, 1024), row2),
        pl.BlockSpec((tm, 1024), row2),
    ]
    return pl.pallas_call(
        functools.partial(_proj_kernel, mla_scale=(B_NOPE + B_ROPE) ** -0.5 * LOG2E),
        grid=(b, lp // tm),
        in_specs=[
            pl.BlockSpec((tm, d), row2),
            pl.BlockSpec(wcat.shape, const2, pipeline_mode=pl.Buffered(1)),
            tab, tab, tab, tab,
            pl.BlockSpec((1, B_Q_LORA), const2),
            pl.BlockSpec((1, B_KV_LORA), const2),
            pl.BlockSpec(wuq.shape, const2, pipeline_mode=pl.Buffered(1)),
            pl.BlockSpec(wukv.shape, const2, pipeline_mode=pl.Buffered(1)),
        ],
        out_specs=out_specs,
        out_shape=out_shapes,
        compiler_params=pltpu.CompilerParams(
            dimension_semantics=("parallel", "parallel"), vmem_limit_bytes=VMEM_LIMIT),
        name="in_proj",
    )(h2d, wcat, cosa, sina, cosm, sinm, qn.reshape(1, -1), kvn.reshape(1, -1), wuq, wukv)


def _float_key(x):
    bits = lax.bitcast_convert_type(x, jnp.int32)
    return bits ^ ((bits >> 31) & 0x7FFFFFFF)


def _key_float(k):
    return lax.bitcast_convert_type(k ^ ((k >> 31) & 0x7FFFFFFF), F32)


def _finish_heads(acc_ref_g, n_heads, bq):
    outs = []
    for hh in range(n_heads):
        a = acc_ref_g[:, hh * bq:(hh + 1) * bq]
        o_t = a / a[B_V:B_V + 1, :]
        outs.append(jnp.transpose(o_t))
    return outs


def _pack_pairs(blocks):
    lane = lax.broadcasted_iota(jnp.int32, blocks[0].shape, 1)
    pairs = []
    for j in range(len(blocks) // 2):
        hi = pltpu.roll(blocks[2 * j + 1], 64, axis=1)
        pairs.append(jnp.where(lane < 64, blocks[2 * j], hi))
    return jnp.concatenate(pairs, axis=1) if len(pairs) > 1 else pairs[0]


def _pipelined_steps(n_steps, stage_a, stage_b, buf_refs, carry):
    def pair(t, c):
        carry, aux = c
        j = 2 * t
        aux1 = stage_a(j + 1, buf_refs[1])
        carry = stage_b(j, buf_refs[0], carry, aux)
        aux2 = stage_a(j + 2, buf_refs[0])
        carry = stage_b(j + 1, buf_refs[1], carry, aux1)
        return carry, aux2

    n_pairs = (n_steps - 1) // 2
    n_quads = n_pairs // 2
    state = lax.fori_loop(0, n_quads, lambda t, c: pair(2 * t + 1, pair(2 * t, c)),
                          (carry, stage_a(0, buf_refs[0])))
    carry, aux = lax.fori_loop(2 * n_quads, n_pairs, pair, state)
    j = 2 * n_pairs

    def two_left(carry, aux):
        aux1 = stage_a(j + 1, buf_refs[1])
        carry = stage_b(j, buf_refs[0], carry, aux)
        return stage_b(j + 1, buf_refs[1], carry, aux1)

    def one_left(carry, aux):
        return stage_b(j, buf_refs[0], carry, aux)

    return lax.cond(j + 1 < n_steps, two_left, one_left, carry, aux)


def _attention_steps(n_steps, scores_fn, vt_fn, acc_ref, buf_refs, n_cols):
    ng = acc_ref.shape[0]

    def stage_a(j, buf):
        mx = []
        for g, x in enumerate(scores_fn(j)):
            buf[g] = x
            mx.append(jnp.max(x, axis=0, keepdims=True))
        return tuple(mx)

    def stage_b(j, buf, ms, mx):
        out = []
        for g in range(ng):
            m_new = jnp.maximum(ms[g], mx[g])
            p = jnp.exp2(buf[g] - m_new).astype(BF16)
            alpha = jnp.exp2(ms[g] - m_new)
            acc_ref[g] = acc_ref[g] * alpha + jnp.dot(vt_fn(j, g), p, preferred_element_type=F32)
            out.append(m_new)
        return tuple(out)

    m0 = jnp.full((1, n_cols), M_INIT, F32)
    _pipelined_steps(n_steps, stage_a, stage_b, buf_refs, (m0,) * ng)


def _dsa_kernel(iq_ref, iwt_ref, ikd_ref, aq_ref, akd_ref, avt_ref, o_ref,
                st_ref, qi_ref, qa_ref, acc_ref, c_ref, xa_ref, xb_ref, *, bq, w, topk):
    i = pl.program_id(1)
    n_steps = ((i + 1) * bq + (w - 1)) // w
    grp = A_HEADS // A_KV_HEADS

    def key_steps(body, init):
        return lax.fori_loop(0, n_steps, lambda j, c: body(pl.multiple_of(j * w, w), j, c), init)

    lane = lax.broadcasted_iota(jnp.int32, (bq, 128), 1)
    for hd in range(IDX_HEADS):
        keep = (lane < 64) if hd % 2 == 0 else (lane >= 64)
        pair = iq_ref[0, :, 128 * (hd // 2):128 * (hd // 2 + 1)].astype(F32)
        qi_ref[hd * bq:(hd + 1) * bq, :] = jnp.where(keep, pair, 0.0).astype(BF16)
        pair = aq_ref[0, :, 128 * (hd // 2):128 * (hd // 2 + 1)].astype(F32)
        qa_ref[hd // grp, (hd % grp) * bq:(hd % grp + 1) * bq, :] = jnp.where(keep, pair, 0.0).astype(BF16)

    qpos = i * bq + lax.broadcasted_iota(jnp.int32, (1, bq), 1)

    iwt = iwt_ref[0]

    def dots_stage(j, buf):
        kc = ikd_ref[0, pl.ds(pl.multiple_of(j * w, w), w), :]
        for half in range(2):
            buf[half] = lax.dot_general(kc, qi_ref[half * grp * bq:(half + 1) * grp * bq, :], NT_DIMS,
                                        preferred_element_type=F32)
        return ()

    def combine_stage(j, buf, carry, aux):
        start = pl.multiple_of(j * w, w)
        acc = jnp.zeros((w, bq), F32)
        for hd in range(IDX_HEADS):
            x = buf[hd // grp, :, (hd % grp) * bq:(hd % grp + 1) * bq]
            acc = acc + jnp.maximum(x, 0.0) * iwt[hd:hd + 1, :]
        kpos = start + lax.broadcasted_iota(jnp.int32, (w, bq), 0)
        causal = kpos <= qpos
        st_ref[pl.ds(start, w), :] = jnp.where(causal, acc, NEG_INF)
        mn, mx = carry
        lo_s = jnp.where(causal, acc, jnp.inf)
        hi_s = jnp.where(causal, acc, NEG_INF)
        for r in range(w // SUBLANES):
            mn = jnp.minimum(mn, lo_s[r * SUBLANES:(r + 1) * SUBLANES, :])
            mx = jnp.maximum(mx, hi_s[r * SUBLANES:(r + 1) * SUBLANES, :])
        return mn, mx

    mn8, mx8 = _pipelined_steps(n_steps, dots_stage, combine_stage, (xa_ref, xb_ref),
                                (jnp.full((SUBLANES, bq), jnp.inf, F32), jnp.full((SUBLANES, bq), NEG_INF, F32)))
    row_min = jnp.min(mn8, axis=0, keepdims=True)
    row_max = jnp.max(mx8, axis=0, keepdims=True)

    def sweep(fn, init):
        def body(start, j, accs):
            s = st_ref[pl.ds(start, w), :]
            kpos0 = start + lax.broadcasted_iota(jnp.int32, (SUBLANES, bq), 0)
            accs = list(accs)
            for r in range(w // SUBLANES):
                lane_acc = fn(s[r * SUBLANES:(r + 1) * SUBLANES, :], kpos0 + r * SUBLANES, accs[r % 4])
                accs[r % 4] = lane_acc
            return tuple(accs)
        return key_steps(body, (init,) * 4)

    def count_where(pred):
        a = sweep(lambda s, kpos, acc: acc + jnp.where(pred(s, kpos), 1.0, 0.0), jnp.zeros((SUBLANES, bq), F32))
        return jnp.sum((a[0] + a[1]) + (a[2] + a[3]), axis=0, keepdims=True)

    def min_where(pred):
        a = sweep(lambda s, kpos, acc: jnp.minimum(acc, jnp.where(pred(s, kpos), s, jnp.inf)),
                  jnp.full((SUBLANES, bq), jnp.inf, F32))
        return jnp.min(jnp.minimum(jnp.minimum(a[0], a[1]), jnp.minimum(a[2], a[3])), axis=0, keepdims=True)

    def rows8(v):
        return jnp.broadcast_to(v, (SUBLANES, bq))

    kf = float(topk)
    n_causal = (qpos + 1).astype(F32)
    need = n_causal > kf
    all_key = _float_key(jnp.full((1, bq), ALL_KEYS, F32))
    lo0 = jnp.where(need, _float_key(row_min), all_key)
    hi0 = jnp.where(need, _float_key(row_max), all_key)

    def is_active(lo, hi, cnt):
        return (lo < hi) & (cnt > kf + 1.0)

    def bracket_pass(state, use_key_mid):
        lo, hi, cnt = state
        act = is_active(lo, hi, cnt)
        if use_key_mid:
            mid = (lo >> 1) + (hi >> 1) + (((lo & 1) + (hi & 1) + 1) >> 1)
        else:
            fmid = 0.5 * _key_float(lo) + 0.5 * _key_float(hi)
            mid = jnp.minimum(jnp.maximum(_float_key(fmid), lo + 1), hi)
        cb = rows8(_key_float(mid))
        c_mid = count_where(lambda s, kpos: s >= cb)
        ge = c_mid >= kf
        up = act & ge
        dn = act & jnp.logical_not(ge)
        return (jnp.where(up, mid, lo), jnp.where(dn, mid - 1, hi), jnp.where(up, c_mid, cnt))

    def bracket_group(state):
        for p in range(4):
            state = bracket_pass(state, use_key_mid=(p == 3))
        return state

    def any_active(state):
        return jnp.max(is_active(*state).astype(jnp.int32)) > 0

    def key_above(m):
        km = _float_key(m)
        return jnp.where((km >= 0) & (km < 0x00800000), 0x00800000, km + 1)

    def tie_probe(state):
        lo, hi, cnt = state
        act = is_active(lo, hi, cnt)
        lo8 = rows8(_key_float(lo))
        m = min_where(lambda s, kpos: s >= lo8)
        m8 = rows8(m)
        n_gt = count_where(lambda s, kpos: s > m8)
        settled = act & (n_gt < kf)
        step = act & jnp.logical_not(settled)
        km = _float_key(m)
        return (jnp.where(settled, km, jnp.where(step, key_above(m), lo)), jnp.where(settled, km, hi),
                jnp.where(step, n_gt, cnt))

    first_rounds = 4
    state, _ = lax.while_loop(lambda c: any_active(c[0]) & (c[1] < first_rounds),
                              lambda c: (bracket_group(c[0]), c[1] + 1), ((lo0, hi0, n_causal), jnp.int32(0)))
    max_probes = 2
    state, _ = lax.while_loop(lambda c: any_active(c[0]) & (c[1] < max_probes),
                              lambda c: (tie_probe(c[0]), c[1] + 1), (state, jnp.int32(0)))
    lo, hi, cnt = lax.while_loop(any_active, bracket_group, state)
    c_lo = _key_float(lo)
    c_lo8 = rows8(c_lo)
    m_row = min_where(lambda s, kpos: s >= c_lo8)
    c_sel = jnp.where(need & (cnt > kf), _key_float(key_above(m_row)), c_lo)
    c_sel8 = rows8(c_sel)
    n_sel = count_where(lambda s, kpos: s >= c_sel8)
    tie = need & (n_sel != kf)
    c_ref[...] = c_sel

    @pl.when(jnp.max(tie.astype(jnp.int32)) > 0)
    def _():
        want = kf - n_sel
        sub = 128
        lower = jnp.where(lax.broadcasted_iota(jnp.int32, (sub, sub), 0)
                          >= lax.broadcasted_iota(jnp.int32, (sub, sub), 1), 1.0, 0.0).astype(BF16)

        def drop_body(start, j, seen):
            for r in range(w // sub):
                s = st_ref[pl.ds(start + r * sub, sub), :]
                is_m = tie & (s == m_row)
                rank = seen + jnp.dot(lower, jnp.where(is_m, 1.0, 0.0).astype(BF16), preferred_element_type=F32)
                st_ref[pl.ds(start + r * sub, sub), :] = jnp.where(is_m & (rank > want), NEG_INF, s)
                seen = rank[sub - 1:sub, :]
            return seen

        key_steps(drop_body, jnp.zeros((1, bq), F32))
        c_ref[...] = jnp.where(tie, m_row, c_sel)

    acc_ref[...] = jnp.zeros(acc_ref.shape, F32)
    c_fin = c_ref[...]

    def att_scores(j):
        start = pl.multiple_of(j * w, w)
        bias = jnp.where(st_ref[pl.ds(start, w), :] >= c_fin, 0.0, NEG_INF)
        bias4 = jnp.concatenate([bias] * grp, axis=1)
        return [lax.dot_general(akd_ref[0, g, pl.ds(start, w), :], qa_ref[g], NT_DIMS,
                                preferred_element_type=F32) + bias4 for g in range(A_KV_HEADS)]

    _attention_steps(n_steps, att_scores, lambda j, g: avt_ref[0, g, j], acc_ref, (xa_ref, xb_ref), grp * bq)

    blocks = []
    for g in range(A_KV_HEADS):
        blocks += _finish_heads(acc_ref.at[g], grp, bq)
    o_ref[...] = _pack_pairs(blocks).astype(BF16)


def _dsa_attention(iq, iwt, ikd, aq, akd, avt, topk):
    b, lp, _ = iq.shape
    bq = BLOCK_Q
    nq = lp // bq
    w = _key_step(lp)
    grp = A_HEADS // A_KV_HEADS
    return pl.pallas_call(
        functools.partial(_dsa_kernel, bq=bq, w=w, topk=topk),
        grid=(b, nq),
        in_specs=[
            pl.BlockSpec((1, bq, 512), lambda bi, i: (bi, i, 0)),
            pl.BlockSpec((1, IDX_HEADS, bq), lambda bi, i: (bi, 0, i)),
            pl.BlockSpec((1, lp, 128), lambda bi, i: (bi, 0, 0)),
            pl.BlockSpec((1, bq, 512), lambda bi, i: (bi, i, 0)),
            pl.BlockSpec((1, A_KV_HEADS, lp, 128), lambda bi, i: (bi, 0, 0, 0)),
            pl.BlockSpec((1, A_KV_HEADS, lp // w, 128, w), lambda bi, i: (bi, 0, 0, 0, 0)),
        ],
        out_specs=pl.BlockSpec((bq, 512), lambda bi, i: (bi * nq + i, 0)),
        out_shape=jax.ShapeDtypeStruct((b * lp, 512), BF16),
        scratch_shapes=[
            pltpu.VMEM((lp, bq), F32),
            pltpu.VMEM((IDX_HEADS * bq, 128), BF16),
            pltpu.VMEM((A_KV_HEADS, grp * bq, 128), BF16),
            pltpu.VMEM((A_KV_HEADS, 128, grp * bq), F32),
            pltpu.VMEM((1, bq), F32),
            pltpu.VMEM((A_KV_HEADS, w, grp * bq), F32),
            pltpu.VMEM((A_KV_HEADS, w, grp * bq), F32),
        ],
        compiler_params=pltpu.CompilerParams(
            dimension_semantics=("parallel", "arbitrary"), vmem_limit_bytes=VMEM_LIMIT),
        name="dsa_attention",
    )(iq, iwt, ikd, aq, akd, avt)


def _mla_kernel(q_ref, k_ref, vt_ref, o_ref, acc_ref, tri_ref, xa_ref, xb_ref, *, bq):
    i = pl.program_id(2)
    hpg = acc_ref.shape[0]
    acc_ref[...] = jnp.zeros(acc_ref.shape, F32)
    kpos = lax.broadcasted_iota(jnp.int32, (bq, bq), 0)
    qpos = lax.broadcasted_iota(jnp.int32, (bq, bq), 1)
    tri_ref[0] = jnp.zeros((bq, bq), F32)
    tri_ref[1] = jnp.where(kpos <= qpos, 0.0, NEG_INF)

    def scores(j):
        start = pl.multiple_of(j * bq, bq)
        bias = tri_ref[(j == i).astype(jnp.int32)]
        return [lax.dot_general(k_ref[0, hh, pl.ds(start, bq), :], q_ref[0, :, 128 * hh:128 * (hh + 1)],
                                NT_DIMS, preferred_element_type=F32) + bias for hh in range(hpg)]

    _attention_steps(i + 1, scores, lambda j, hh: vt_ref[0, hh, j], acc_ref, (xa_ref, xb_ref), bq)
    blocks = []
    for hh in range(hpg):
        blocks += _finish_heads(acc_ref.at[hh], 1, bq)
    o_ref[...] = _pack_pairs(blocks).astype(BF16)


def _mla_attention(qm, km, vmt):
    b, lp, _ = qm.shape
    bq = _key_step(lp)
    nq = lp // bq
    hpg = 4
    return pl.pallas_call(
        functools.partial(_mla_kernel, bq=bq),
        grid=(b, B_HEADS // hpg, nq),
        in_specs=[
            pl.BlockSpec((1, bq, 128 * hpg), lambda bi, p, i: (bi, i, p)),
            pl.BlockSpec((1, hpg, lp, 128), lambda bi, p, i: (bi, p, 0, 0), pipeline_mode=pl.Buffered(1)),
            pl.BlockSpec((1, hpg, nq, 128, bq), lambda bi, p, i: (bi, p, 0, 0, 0),
                         pipeline_mode=pl.Buffered(1)),
        ],
        out_specs=pl.BlockSpec((bq, B_V * hpg), lambda bi, p, i: (bi * nq + i, p)),
        out_shape=jax.ShapeDtypeStruct((b * lp, B_V * B_HEADS), BF16),
        scratch_shapes=[pltpu.VMEM((hpg, 128, bq), F32), pltpu.VMEM((2, bq, bq), F32),
                        pltpu.VMEM((hpg, bq, bq), F32), pltpu.VMEM((hpg, bq, bq), F32)],
        compiler_params=pltpu.CompilerParams(
            dimension_semantics=("parallel", "parallel", "arbitrary"), vmem_limit_bytes=VMEM_LIMIT),
        name="mla_attention",
    )(qm, km, vmt)


def _merge_kernel(h_ref, oa_ref, ob_ref, sa_ref, sb_ref, wa_ref, wb_ref, wo_ref, g_ref, b_ref, o_ref,
                  *, alpha):
    pa = jnp.dot(oa_ref[...], wa_ref[...], preferred_element_type=F32)
    pb = jnp.dot(ob_ref[...], wb_ref[...], preferred_element_type=F32)
    mixed = sa_ref[...].astype(F32) * pa + sb_ref[...].astype(F32) * pb
    y = alpha * h_ref[...] + jnp.dot(mixed.astype(BF16), wo_ref[...], preferred_element_type=F32)
    o_ref[...] = _layer_norm_rows(y, g_ref[...], b_ref[...])


def _merge_ln(h2d, oa, ob, sa, sb, wa, wb, wo, g, b, alpha):
    t, d = h2d.shape
    tm = _pick_tile(t, (512, 256, 128, 64, 32, 16, 8))
    row = lambda i: (i, 0)
    const = lambda i: (0, 0)
    return pl.pallas_call(
        functools.partial(_merge_kernel, alpha=alpha),
        grid=(t // tm,),
        in_specs=[
            pl.BlockSpec((tm, d), row),
            pl.BlockSpec((tm, oa.shape[1]), row),
            pl.BlockSpec((tm, ob.shape[1]), row),
            pl.BlockSpec((tm, d), row),
            pl.BlockSpec((tm, d), row),
            pl.BlockSpec(wa.shape, const),
            pl.BlockSpec(wb.shape, const),
            pl.BlockSpec(wo.shape, const),
            pl.BlockSpec((1, d), const),
            pl.BlockSpec((1, d), const),
        ],
        out_specs=pl.BlockSpec((tm, d), row),
        out_shape=jax.ShapeDtypeStruct((t, d), F32),
        compiler_params=pltpu.CompilerParams(
            dimension_semantics=("parallel",), vmem_limit_bytes=VMEM_LIMIT),
        name="merge_ln",
    )(h2d, oa, ob, sa, sb, wa, wb, wo, g.reshape(1, d), b.reshape(1, d))


def kernel(x, meta_tokens, ln_g, ln_b, ffn1_w13, ffn1_w2, w_in, mla_q_norm, mla_kv_norm, mla_w_uq,
           mla_w_ukv, w_branch_a, w_branch_b, w_out, ffn2_w13, ffn2_w2):
    b, s, d = x.shape
    depth = ln_g.shape[0]
    alpha = (2 * depth) ** 0.25
    total = s + N_META
    lp = -(-total // BLOCK_Q) * BLOCK_Q
    topk = min(TOPK_MAX, s // 4)
    meta = jnp.broadcast_to(meta_tokens[None].astype(x.dtype), (b, N_META, d))
    h = jnp.concatenate([meta, x, jnp.zeros((b, lp - total, d), x.dtype)], axis=1).reshape(b * lp, d)
    tables = _rope_tables(lp)
    t = b * lp
    for l in range(depth):
        h2 = _ffn_ln(h, ffn1_w13[l].astype(BF16), ffn1_w2[l].astype(BF16),
                     ln_g[l, 0], ln_b[l, 0], alpha)
        wcat, wuq, wukv = _prep_proj_weights(w_in[l], mla_w_uq[l], mla_w_ukv[l])
        (aq, akd, avt, iq, ikd, iw, qm, km, vmt, sa, sb) = _projection(
            h2, b, wcat, wuq, wukv, mla_q_norm[l], mla_kv_norm[l], tables)
        iwt = jnp.swapaxes(iw[:, :, :IDX_HEADS], 1, 2)
        o_a = _dsa_attention(iq, iwt, ikd, aq, akd, avt, topk)
        o_b = _mla_attention(qm, km, vmt)
        h3 = _merge_ln(h2, o_a, o_b, sa, sb,
                       w_branch_a[l].astype(BF16), w_branch_b[l].astype(BF16), w_out[l].astype(BF16),
                       ln_g[l, 1], ln_b[l, 1], alpha)
        h = _ffn_ln(h3, ffn2_w13[l].astype(BF16), ffn2_w2[l].astype(BF16), ln_g[l, 2], ln_b[l, 2], alpha)
    return h.reshape(b, lp, d)[:, N_META:N_META + s]
```
